```python
import math
import jax, jax.numpy as jnp
from jax import lax
import numpy as np

D_MODEL = 1024
BATCH = 4
SEQ = 8192
DEPTH = 2
DEC_BATCH = 8
DEC_SEQ = 32
PAST_LEN = 1024

CHUNK = 64
HEAD_DIM = 64
A_HEADS = 6
A_BACK_CHUNKS = 8
A_WIDTH = A_HEADS * HEAD_DIM
REL_CLIP = 128
B_GROUP_CH = 16
B_GROUPS = 16
B_WIDTH = B_GROUPS * B_GROUP_CH
B_STATE = 64
DT_MIN = 1e-3
DT_MAX = 1e-1
C_KV_HEADS = 2
C_GROUP = 3
C_WIDTH = C_KV_HEADS * C_GROUP * HEAD_DIM
C_KV_WIDTH = C_KV_HEADS * HEAD_DIM
C_WINDOW = 128
C_BACK_CHUNKS = C_WINDOW // CHUNK
ROPE_THETA = 500000.0
ROPE_DIM = HEAD_DIM // 4
D_MIX = A_WIDTH + B_WIDTH + C_WIDTH
D_IN = 3 * A_WIDTH + B_WIDTH + C_WIDTH + 2 * C_KV_WIDTH
SPLITS = (A_WIDTH, 2 * A_WIDTH, 3 * A_WIDTH, 3 * A_WIDTH + B_WIDTH,
          3 * A_WIDTH + B_WIDTH + C_WIDTH, 3 * A_WIDTH + B_WIDTH + C_WIDTH + C_KV_WIDTH)
D_FF = 2816
EPS = 1e-6
NEG_INF = -1e30
ATTN_SCALE = HEAD_DIM ** -0.5

kernel_name = "hybrid_streaming_encoder_step"


def rmsnorm(x, g):
    xf = x.astype(jnp.float32)
    y = xf * lax.rsqrt(jnp.mean(xf * xf, axis=-1, keepdims=True) + EPS)
    return (y * g.astype(jnp.float32)).astype(x.dtype)


def swiglu(x, w_up, w_down):
    gate, up = jnp.split(x @ w_up, 2, axis=-1)
    return (jax.nn.silu(gate) * up) @ w_down


def rope(x, pos):
    half = ROPE_DIM // 2
    inv_freq = ROPE_THETA ** (-jnp.arange(half, dtype=jnp.float32) / half)
    ang = pos.astype(jnp.float32)[:, None] * inv_freq[None, :]
    bshape = (pos.shape[0],) + (1,) * (x.ndim - 3) + (half,)
    cos = jnp.cos(ang).reshape(bshape)
    sin = jnp.sin(ang).reshape(bshape)
    xf = x.astype(jnp.float32)
    x1, x2, rest = xf[..., :half], xf[..., half:ROPE_DIM], xf[..., ROPE_DIM:]
    return jnp.concatenate([x1 * cos - x2 * sin, x2 * cos + x1 * sin, rest], axis=-1).astype(x.dtype)


def rel_bias(table, q_off, n_q, n_k):
    rel = q_off + jnp.arange(n_q)[:, None] - jnp.arange(n_k)[None, :]
    idx = jnp.clip(rel, -REL_CLIP, REL_CLIP) + REL_CLIP
    return table.astype(jnp.float32)[:, idx][:, None]


def band_softmax(q, k, v, bias, mask, sink):
    s = jnp.einsum('bnqhgd,bnkhd->bnhgqk', q, k, preferred_element_type=jnp.float32) * ATTN_SCALE
    s = jnp.where(mask, s + bias, NEG_INF)
    if sink is None:
        p = jax.nn.softmax(s, axis=-1)
    else:
        sk = sink.astype(jnp.float32)[None, None, :, :, None, None]
        m = jnp.maximum(jnp.max(s, axis=-1, keepdims=True), sk)
        e = jnp.exp(s - m)
        p = e / (jnp.sum(e, axis=-1, keepdims=True) + jnp.exp(sk - m))
    return jnp.einsum('bnhgqk,bnkhd->bnqhgd', p.astype(v.dtype), v)


def chunk_band(x, n_back):
    b, t, h, d = x.shape
    n = t // CHUNK
    xp = jnp.pad(x.reshape(b, n, CHUNK, h, d), ((0, 0), (n_back, 0), (0, 0), (0, 0), (0, 0)))
    return jnp.concatenate([xp[:, j:j + n] for j in range(n_back + 1)], axis=2)


def attend_prompt(q, k, v, n_back, table, sink):
    b, t = q.shape[0], q.shape[1]
    n = t // CHUNK
    kb, vb = chunk_band(k, n_back), chunk_band(v, n_back)
    n_k = kb.shape[2]
    qc = q.reshape((b, n, CHUNK) + q.shape[2:])
    kpos = jnp.arange(n)[:, None] * CHUNK + jnp.arange(n_k)[None, :] - n_back * CHUNK
    mask = (kpos >= 0)[None, :, None, None, None, :]
    bias = 0.0 if table is None else rel_bias(table, n_back * CHUNK, CHUNK, n_k)
    return band_softmax(qc, kb, vb, bias, mask, sink).reshape(b, t, -1)


def attend_sample(q, k, v, k_cache, v_cache, table, sink):
    b, t = q.shape[0], q.shape[1]
    w = k_cache.shape[1]
    kb = jnp.concatenate([k_cache.astype(k.dtype), k], axis=1)
    vb = jnp.concatenate([v_cache.astype(v.dtype), v], axis=1)
    bias = 0.0 if table is None else rel_bias(table, w, t, w + t)
    o = band_softmax(q[:, None], kb[:, None], vb[:, None], bias, True, sink)
    return o.reshape(b, t, -1), kb[:, -w:], vb[:, -w:]


def _ssm_combine(e1, e2):
    a1r, a1i, b1r, b1i = e1
    a2r, a2i, b2r, b2i = e2
    return (a2r * a1r - a2i * a1i, a2r * a1i + a2i * a1r,
            a2r * b1r - a2i * b1i + b2r, a2r * b1i + a2i * b1r + b2i)


def s5_scan(u, h0_re, h0_im, a_re, a_im, log_dt, b_re, b_im, c_re, c_im, d_skip):
    f32 = jnp.float32
    uf = u.astype(f32)
    ar, ai = a_re.astype(f32), a_im.astype(f32)
    dt = jnp.exp(log_dt.astype(f32))[:, None]
    mag = jnp.exp(ar * dt)
    lr, li = mag * jnp.cos(ai * dt), mag * jnp.sin(ai * dt)
    den = ar * ar + ai * ai
    zr = ((lr - 1.0) * ar + li * ai) / den
    zi = (li * ar - (lr - 1.0) * ai) / den
    br, bi = b_re.astype(f32), b_im.astype(f32)
    bbr = zr[..., None] * br - zi[..., None] * bi
    bbi = zr[..., None] * bi + zi[..., None] * br
    vr = jnp.einsum('btgh,gph->btgp', uf, bbr)
    vi = jnp.einsum('btgh,gph->btgp', uf, bbi)
    h0r, h0i = h0_re.astype(f32), h0_im.astype(f32)
    vr = vr.at[:, 0].add(lr * h0r - li * h0i)
    vi = vi.at[:, 0].add(lr * h0i + li * h0r)
    lam_r = jnp.broadcast_to(lr, vr.shape)
    lam_i = jnp.broadcast_to(li, vi.shape)
    _, _, xr, xi = lax.associative_scan(_ssm_combine, (lam_r, lam_i, vr, vi), axis=1)
    y = (jnp.einsum('btgp,ghp->btgh', xr, c_re.astype(f32))
         - jnp.einsum('btgp,ghp->btgh', xi, c_im.astype(f32))
         + d_skip.astype(f32) * uf)
    return y, xr[:, -1], xi[:, -1]


def trunk_layer(x, pos, cache, p):
    b, t, _ = x.shape
    g = p['norm_g']
    h = x + 0.5 * rmsnorm(swiglu(rmsnorm(x, g[0]), p['w_ffn_up'][0], p['w_ffn_down'][0]), g[1])
    hn = rmsnorm(h, g[2])
    qa, ka, va, ub, qc, kc, vc = jnp.split(hn @ p['w_in'], SPLITS, axis=-1)
    qa = qa.reshape(b, t, A_HEADS, 1, HEAD_DIM)
    ka = ka.reshape(b, t, A_HEADS, HEAD_DIM)
    va = va.reshape(b, t, A_HEADS, HEAD_DIM)
    qc = rope(qc.reshape(b, t, C_KV_HEADS, C_GROUP, HEAD_DIM), pos)
    kc = rope(kc.reshape(b, t, C_KV_HEADS, HEAD_DIM), pos)
    vc = vc.reshape(b, t, C_KV_HEADS, HEAD_DIM)
    ub = ub.reshape(b, t, B_GROUPS, B_GROUP_CH)
    if cache is None:
        oa = attend_prompt(qa, ka, va, A_BACK_CHUNKS, p['a_rel_bias'], None)
        oc = attend_prompt(qc, kc, vc, C_BACK_CHUNKS, None, p['c_sink'])
        wa = min(A_BACK_CHUNKS * CHUNK, t)
        wc = min(C_WINDOW, t)
        nak, nav = ka[:, t - wa:], va[:, t - wa:]
        nck, ncv = kc[:, t - wc:], vc[:, t - wc:]
        h0_re = jnp.zeros((b, B_GROUPS, B_STATE), jnp.float32)
        h0_im = jnp.zeros((b, B_GROUPS, B_STATE), jnp.float32)
    else:
        a_k, a_v, c_k, c_v, h0_re, h0_im = cache
        oa, nak, nav = attend_sample(qa, ka, va, a_k, a_v, p['a_rel_bias'], None)
        oc, nck, ncv = attend_sample(qc, kc, vc, c_k, c_v, None, p['c_sink'])
    y_ssm, s_re, s_im = s5_scan(ub, h0_re, h0_im, p['ssm_a_re'], p['ssm_a_im'], p['ssm_log_dt'],
                                p['ssm_b_re'], p['ssm_b_im'], p['ssm_c_re'], p['ssm_c_im'], p['ssm_d'])
    gl = jax.nn.gelu(y_ssm.reshape(b, t, B_WIDTH).astype(h.dtype))
    g1, g2 = jnp.split(gl @ p['w_glu'], 2, axis=-1)
    ob = g1 * jax.nn.sigmoid(g2)
    gb = p['branch_norm_g']
    mixed = jnp.concatenate([rmsnorm(oa, gb[:A_WIDTH]),
                             rmsnorm(ob, gb[A_WIDTH:A_WIDTH + B_WIDTH]),
                             rmsnorm(oc, gb[A_WIDTH + B_WIDTH:])], axis=-1)
    h = h + rmsnorm(mixed @ p['w_out'], g[3])
    h = h + 0.5 * rmsnorm(swiglu(rmsnorm(h, g[4]), p['w_ffn_up'][1], p['w_ffn_down'][1]), g[5])
    return h, (nak, nav, nck, ncv, s_re, s_im)


def setup_inputs(seed: int = 0) -> dict:
    key = jax.random.key(seed)
    ks = jax.random.split(key, 26)
    f32 = jnp.float32

    def nrm(k, shape, scale):
        return scale * jax.random.normal(k, shape, f32)

    wa = min(A_BACK_CHUNKS * CHUNK, PAST_LEN)
    wc = min(C_WINDOW, PAST_LEN)
    n_idx = jnp.arange(B_STATE, dtype=f32)
    return {
        'x_prompt': nrm(ks[0], (BATCH, SEQ, D_MODEL), 1.0),
        'x_sample': nrm(ks[1], (DEC_BATCH, DEC_SEQ, D_MODEL), 1.0),
        'cache_a_k': nrm(ks[2], (DEPTH, DEC_BATCH, wa, A_HEADS, HEAD_DIM), 1.0),
        'cache_a_v': nrm(ks[3], (DEPTH, DEC_BATCH, wa, A_HEADS, HEAD_DIM), 1.0),
        'cache_c_k': nrm(ks[4], (DEPTH, DEC_BATCH, wc, C_KV_HEADS, HEAD_DIM), 1.0),
        'cache_c_v': nrm(ks[5], (DEPTH, DEC_BATCH, wc, C_KV_HEADS, HEAD_DIM), 1.0),
        'state_ssm_re': nrm(ks[6], (DEPTH, DEC_BATCH, B_GROUPS, B_STATE), 0.5),
        'state_ssm_im': nrm(ks[7], (DEPTH, DEC_BATCH, B_GROUPS, B_STATE), 0.5),
        'norm_g': 1.0 + nrm(ks[8], (DEPTH, 6, D_MODEL), 0.05),
        'branch_norm_g': 1.0 + nrm(ks[9], (DEPTH, D_MIX), 0.05),
        'w_ffn_up': nrm(ks[10], (DEPTH, 2, D_MODEL, 2 * D_FF), D_MODEL ** -0.5),
        'w_ffn_down': nrm(ks[11], (DEPTH, 2, D_FF, D_MODEL), D_FF ** -0.5),
        'w_in': nrm(ks[12], (DEPTH, D_MODEL, D_IN), D_MODEL ** -0.5),
        'w_out': nrm(ks[13], (DEPTH, D_MIX, D_MODEL), D_MIX ** -0.5),
        'a_rel_bias': nrm(ks[14], (DEPTH, A_HEADS, 2 * REL_CLIP + 1), 0.5),
        'c_sink': nrm(ks[15], (DEPTH, C_KV_HEADS, C_GROUP), 1.0),
        'ssm_a_re': -0.5 + nrm(ks[16], (DEPTH, B_GROUPS, B_STATE), 0.01),
        'ssm_a_im': math.pi * n_idx + nrm(ks[17], (DEPTH, B_GROUPS, B_STATE), 0.01),
        'ssm_log_dt': jax.random.uniform(ks[18], (DEPTH, B_GROUPS), f32,
                                         minval=math.log(DT_MIN), maxval=math.log(DT_MAX)),
        'ssm_b_re': nrm(ks[19], (DEPTH, B_GROUPS, B_STATE, B_GROUP_CH), (2 * B_GROUP_CH) ** -0.5),
        'ssm_b_im': nrm(ks[20], (DEPTH, B_GROUPS, B_STATE, B_GROUP_CH), (2 * B_GROUP_CH) ** -0.5),
        'ssm_c_re': nrm(ks[21], (DEPTH, B_GROUPS, B_GROUP_CH, B_STATE), B_STATE ** -0.5),
        'ssm_c_im': nrm(ks[22], (DEPTH, B_GROUPS, B_GROUP_CH, B_STATE), B_STATE ** -0.5),
        'ssm_d': nrm(ks[23], (DEPTH, B_GROUPS, B_GROUP_CH), 1.0),
        'w_glu': nrm(ks[24], (DEPTH, B_WIDTH, 2 * B_WIDTH), B_WIDTH ** -0.5),
    }


def reference(x_prompt, x_sample, cache_a_k, cache_a_v, cache_c_k, cache_c_v, state_ssm_re, state_ssm_im,
              norm_g, branch_norm_g, w_ffn_up, w_ffn_down, w_in, w_out, a_rel_bias, c_sink,
              ssm_a_re, ssm_a_im, ssm_log_dt, ssm_b_re, ssm_b_im, ssm_c_re, ssm_c_im, ssm_d, w_glu):
    pos_p = jnp.arange(x_prompt.shape[1])
    pos_s = PAST_LEN + jnp.arange(x_sample.shape[1])
    yp, ys = x_prompt, x_sample
    st_p, st_s = [], []
    for l in range(DEPTH):
        p = {'norm_g': norm_g[l], 'branch_norm_g': branch_norm_g[l], 'w_ffn_up': w_ffn_up[l],
             'w_ffn_down': w_ffn_down[l], 'w_in': w_in[l], 'w_out': w_out[l], 'a_rel_bias': a_rel_bias[l],
             'c_sink': c_sink[l], 'ssm_a_re': ssm_a_re[l], 'ssm_a_im': ssm_a_im[l], 'ssm_log_dt': ssm_log_dt[l],
             'ssm_b_re': ssm_b_re[l], 'ssm_b_im': ssm_b_im[l], 'ssm_c_re': ssm_c_re[l], 'ssm_c_im': ssm_c_im[l],
             'ssm_d': ssm_d[l], 'w_glu': w_glu[l]}
        yp, sp = trunk_layer(yp, pos_p, None, p)
        ys, ss = trunk_layer(ys, pos_s, (cache_a_k[l], cache_a_v[l], cache_c_k[l], cache_c_v[l],
                                         state_ssm_re[l], state_ssm_im[l]), p)
        st_p.append(sp)
        st_s.append(ss)
    a_k_p = jnp.stack([s[0] for s in st_p])
    a_v_p = jnp.stack([s[1] for s in st_p])
    c_k_p = jnp.stack([s[2] for s in st_p])
    c_v_p = jnp.stack([s[3] for s in st_p])
    ssm_re_p = jnp.stack([s[4] for s in st_p])
    ssm_im_p = jnp.stack([s[5] for s in st_p])
    a_k_s = jnp.stack([s[0] for s in st_s])
    a_v_s = jnp.stack([s[1] for s in st_s])
    c_k_s = jnp.stack([s[2] for s in st_s])
    c_v_s = jnp.stack([s[3] for s in st_s])
    ssm_re_s = jnp.stack([s[4] for s in st_s])
    ssm_im_s = jnp.stack([s[5] for s in st_s])
    return (yp, ys, a_k_p, a_v_p, c_k_p, c_v_p, ssm_re_p, ssm_im_p,
            a_k_s, a_v_s, c_k_s, c_v_s, ssm_re_s, ssm_im_s)
```

```python
import functools
import math

import jax
import jax.numpy as jnp
from jax import lax
from jax.experimental import pallas as pl
from jax.experimental.pallas import tpu as pltpu

F32 = jnp.float32
BF16 = jnp.bfloat16

D_MODEL = 1024
HEAD_DIM = 64
CHUNK = 64
A_HEADS = 6
A_WIDTH = A_HEADS * HEAD_DIM
A_BAND = 8 * CHUNK
REL_CLIP = 128
B_GROUPS = 16
B_GROUP_CH = 16
B_WIDTH = B_GROUPS * B_GROUP_CH
B_STATE = 64
N_STATE = B_GROUPS * B_STATE
C_KV_HEADS = 2
C_GROUP = 3
C_HEADS = C_KV_HEADS * C_GROUP
C_WIDTH = C_HEADS * HEAD_DIM
C_KV_WIDTH = C_KV_HEADS * HEAD_DIM
C_BAND = 128
ROPE_THETA = 500000.0
ROPE_DIM = HEAD_DIM // 4
D_MIX = A_WIDTH + B_WIDTH + C_WIDTH
D_IN = 3 * A_WIDTH + B_WIDTH + C_WIDTH + 2 * C_KV_WIDTH
D_FF = 2816
EPS = 1e-6
NEG_INF = -1e30
ATTN_SCALE = HEAD_DIM ** -0.5
PAST_LEN = 1024

LANES = 128
SUBLANES = 8
VMEM_LIMIT_BYTES = 56 * 1024 * 1024

_O_QA, _O_KA, _O_VA = 0, A_WIDTH, 2 * A_WIDTH
_O_UB = 3 * A_WIDTH
_O_QC = _O_UB + B_WIDTH
_O_KC = _O_QC + C_WIDTH
_O_VC = _O_KC + C_KV_WIDTH


def _rms(x, g):
    return x * lax.rsqrt(jnp.mean(x * x, axis=-1, keepdims=True) + EPS) * g


def _resident(shape):
    nd = len(shape)
    return pl.BlockSpec(shape, lambda *_: (0,) * nd, pipeline_mode=pl.Buffered(1))


def _ffn_kernel(x_ref, g_ref, wup_ref, wdn_ref, o_ref):
    x = x_ref[...]
    xn = _rms(x, g_ref[0:1, :]).astype(BF16)
    mid = jnp.dot(xn, wup_ref[...], preferred_element_type=F32)
    gate, up = mid[:, :D_FF], mid[:, D_FF:]
    act = (gate * jax.nn.sigmoid(gate) * up).astype(BF16)
    y = jnp.dot(act, wdn_ref[...], preferred_element_type=F32)
    o_ref[...] = x + 0.5 * _rms(y, g_ref[1:2, :])


def _ffn(x2d, g2, wup, wdn, tm):
    n = x2d.shape[0]
    return pl.pallas_call(
        _ffn_kernel,
        grid=(n // tm,),
        in_specs=[pl.BlockSpec((tm, D_MODEL), lambda i: (i, 0)),
                  _resident((2, D_MODEL)),
                  _resident((D_MODEL, 2 * D_FF)),
                  _resident((D_FF, D_MODEL))],
        out_specs=pl.BlockSpec((tm, D_MODEL), lambda i: (i, 0)),
        out_shape=jax.ShapeDtypeStruct((n, D_MODEL), F32),
        compiler_params=pltpu.CompilerParams(dimension_semantics=("parallel",),
                                             vmem_limit_bytes=VMEM_LIMIT_BYTES),
        name="ffn",
    )(x2d, g2, wup, wdn)


def _rope_slab(x, cos, sin_hi, sin_lo):
    half = ROPE_DIM // 2
    return (x * cos + pltpu.roll(x, LANES - half, 1) * sin_hi + pltpu.roll(x, half, 1) * sin_lo)


def _proj_kernel(h_ref, g_ref, win_ref, rope_ref,
                 qa_ref, ka_ref, va_ref, ub_ref, qc_ref, kc_ref, vc_ref):
    hn = _rms(h_ref[...], g_ref[...]).astype(BF16)
    p = jnp.dot(hn, win_ref[...], preferred_element_type=F32)
    cos, sin_hi, sin_lo = rope_ref[0], rope_ref[1], rope_ref[2]
    qa_ref[...] = (p[:, _O_QA:_O_QA + A_WIDTH] * ATTN_SCALE).astype(BF16)
    ka_ref[...] = p[:, _O_KA:_O_KA + A_WIDTH]
    va_ref[...] = p[:, _O_VA:_O_VA + A_WIDTH]
    for s in range(B_WIDTH // LANES):
        ub_ref[s] = p[:, _O_UB + s * LANES:_O_UB + (s + 1) * LANES]
    for s in range(C_WIDTH // LANES):
        q = p[:, _O_QC + s * LANES:_O_QC + (s + 1) * LANES]
        qc_ref[:, s * LANES:(s + 1) * LANES] = (_rope_slab(q, cos, sin_hi, sin_lo) * ATTN_SCALE).astype(BF16)
    kc_ref[...] = _rope_slab(p[:, _O_KC:_O_KC + C_KV_WIDTH], cos, sin_hi, sin_lo)
    vc_ref[...] = p[:, _O_VC:_O_VC + C_KV_WIDTH]


def _proj(h, g, win, rope_tab, tm):
    b, t, _ = h.shape
    row = lambda w: pl.BlockSpec((None, tm, w), lambda bi, i: (bi, i, 0))
    n_ub = B_WIDTH // LANES
    return pl.pallas_call(
        _proj_kernel,
        grid=(b, t // tm),
        in_specs=[row(D_MODEL),
                  _resident((1, D_MODEL)),
                  _resident((D_MODEL, D_IN)),
                  pl.BlockSpec((3, tm, LANES), lambda bi, i: (0, i, 0))],
        out_specs=[row(A_WIDTH), row(A_WIDTH), row(A_WIDTH),
                   pl.BlockSpec((None, n_ub, tm, LANES), lambda bi, i: (bi, 0, i, 0)),
                   row(C_WIDTH), row(C_KV_WIDTH), row(C_KV_WIDTH)],
        out_shape=[jax.ShapeDtypeStruct((b, t, A_WIDTH), BF16),
                   jax.ShapeDtypeStruct((b, t, A_WIDTH), F32),
                   jax.ShapeDtypeStruct((b, t, A_WIDTH), F32),
                   jax.ShapeDtypeStruct((b, n_ub, t, LANES), F32),
                   jax.ShapeDtypeStruct((b, t, C_WIDTH), BF16),
                   jax.ShapeDtypeStruct((b, t, C_KV_WIDTH), F32),
                   jax.ShapeDtypeStruct((b, t, C_KV_WIDTH), F32)],
        compiler_params=pltpu.CompilerParams(dimension_semantics=("parallel", "parallel"),
                                             vmem_limit_bytes=VMEM_LIMIT_BYTES),
        name="proj",
    )(h, g, win, rope_tab)


def _ssm_prep_kernel(ar_ref, ai_ref, ldt_ref, br_ref, bi_ref, lr_ref, li_ref, bbr_ref, bbi_ref):
    ar, ai = ar_ref[...], ai_ref[...]
    dt = jnp.exp(ldt_ref[...])
    mag = jnp.exp(ar * dt)
    lr, li = mag * jnp.cos(ai * dt), mag * jnp.sin(ai * dt)
    den = ar * ar + ai * ai
    zr = ((lr - 1.0) * ar + li * ai) / den
    zi = (li * ar - (lr - 1.0) * ai) / den
    br, bi = br_ref[...], bi_ref[...]
    lr_ref[...] = lr
    li_ref[...] = li
    bbr_ref[...] = zr * br - zi * bi
    bbi_ref[...] = zr * bi + zi * br


def _ssm_prep(a_re, a_im, log_dt, b_re, b_im):
    rep = lambda a: jnp.repeat(a, B_GROUP_CH, axis=0)
    rows = B_GROUPS * B_GROUP_CH
    bt = lambda b: jnp.transpose(b, (0, 2, 1)).reshape(rows, B_STATE)
    sds = jax.ShapeDtypeStruct((rows, B_STATE), F32)
    lr, li, bbr, bbi = pl.pallas_call(
        _ssm_prep_kernel, out_shape=[sds, sds, sds, sds], name="ssm_prep",
    )(rep(a_re), rep(a_im), rep(log_dt[:, None]), bt(b_re), bt(b_im))
    return lr[::B_GROUP_CH], li[::B_GROUP_CH], bbr, bbi


def _block_diag(m, rows_per_group, cols_per_group):
    eye = jnp.eye(B_GROUPS, dtype=m.dtype)
    full = m[:, :, None, :] * eye[:, None, :, None]
    return full.reshape(B_GROUPS * rows_per_group, B_GROUPS * cols_per_group)


def _mixer_kernel(sink_ref,
                  h_ref, qa_ref, kap_ref, kac_ref, vap_ref, vac_ref, bias_ref,
                  qc_ref, kcp_ref, kcc_ref, vcp_ref, vcc_ref,
                  ub_ref, st0_ref, lam_ref, wb_ref, wc_ref, dsk_ref, wglu_ref,
                  gb_ref, wout_ref, g3_ref,
                  o_ref, st_ref,
                  kbuf, vbuf, kcbuf, vcbuf, mix_scr, u_scr, v_scr, x_scr, nb_scr, carry_scr,
                  *, rows, chunk, masked):
    i = pl.program_id(1)
    n_chunks = rows // chunk
    band_a = A_BAND + chunk
    band_c = C_BAND + chunk

    kbuf[0:A_BAND, :] = kap_ref[...].astype(BF16)
    kbuf[A_BAND:A_BAND + rows, :] = kac_ref[...].astype(BF16)
    vbuf[0:A_BAND, :] = vap_ref[...].astype(BF16)
    vbuf[A_BAND:A_BAND + rows, :] = vac_ref[...].astype(BF16)
    kcbuf[0:C_BAND, :] = kcp_ref[...].astype(BF16)
    kcbuf[C_BAND:C_BAND + rows, :] = kcc_ref[...].astype(BF16)
    vcbuf[0:C_BAND, :] = vcp_ref[...].astype(BF16)
    vcbuf[C_BAND:C_BAND + rows, :] = vcc_ref[...].astype(BF16)

    lane = lax.broadcasted_iota(jnp.int32, (1, LANES), 1)
    first_head = lane < HEAD_DIM
    contract_last = (((1,), (1,)), ((), ()))

    def chunk_body(c, carry):
        r0 = pl.multiple_of(c * chunk, chunk)
        q_pos = i * rows + c * chunk
        if masked:
            valid_a = lax.broadcasted_iota(jnp.int32, (1, band_a), 1) >= A_BAND - q_pos
        slabs = []
        for s in range(A_WIDTH // LANES):
            q = qa_ref[pl.ds(r0, chunk), s * LANES:(s + 1) * LANES]
            kb = kbuf[pl.ds(r0, band_a), s * LANES:(s + 1) * LANES]
            vb = vbuf[pl.ds(r0, band_a), s * LANES:(s + 1) * LANES]
            outs = []
            for hh in range(2):
                qm = jnp.where(first_head if hh == 0 else jnp.logical_not(first_head), q, jnp.zeros_like(q))
                sc = lax.dot_general(qm, kb, contract_last, preferred_element_type=F32)
                sc = sc + bias_ref[2 * s + hh]
                if masked:
                    sc = jnp.where(valid_a, sc, NEG_INF)
                m = jnp.max(sc, axis=-1, keepdims=True)
                e = jnp.exp(sc - m)
                den = jnp.sum(e, axis=-1, keepdims=True)
                o = jnp.dot(e.astype(BF16), vb, preferred_element_type=F32)
                outs.append(o * (1.0 / den))
            slabs.append(jnp.where(first_head, outs[0], outs[1]))
        oa = jnp.concatenate(slabs, axis=-1)
        mix_scr[pl.ds(r0, chunk), 0:A_WIDTH] = _rms(oa, gb_ref[:, 0:A_WIDTH]).astype(BF16)

        if masked:
            valid_c = lax.broadcasted_iota(jnp.int32, (1, band_c), 1) >= C_BAND - q_pos
        kb = kcbuf[pl.ds(r0, band_c), :]
        vb = vcbuf[pl.ds(r0, band_c), :]
        slabs = []
        for s in range(C_WIDTH // LANES):
            q = qc_ref[pl.ds(r0, chunk), s * LANES:(s + 1) * LANES]
            outs = []
            for hh in range(2):
                sink = sink_ref[hh * C_GROUP + s]
                qm = jnp.where(first_head if hh == 0 else jnp.logical_not(first_head), q, jnp.zeros_like(q))
                sc = lax.dot_general(qm, kb, contract_last, preferred_element_type=F32)
                if masked:
                    sc = jnp.where(valid_c, sc, NEG_INF)
                m = jnp.maximum(jnp.max(sc, axis=-1, keepdims=True), sink)
                e = jnp.exp(sc - m)
                den = jnp.sum(e, axis=-1, keepdims=True) + jnp.exp(sink - m)
                o = jnp.dot(e.astype(BF16), vb, preferred_element_type=F32)
                outs.append(o * (1.0 / den))
            slabs.append(jnp.where(first_head, outs[0], outs[1]))
        oc = jnp.concatenate(slabs, axis=-1)
        mix_scr[pl.ds(r0, chunk), A_WIDTH + B_WIDTH:D_MIX] = _rms(oc, gb_ref[:, A_WIDTH + B_WIDTH:D_MIX]).astype(BF16)
        return carry

    lax.fori_loop(0, n_chunks, chunk_body, 0)

    seg = rows // SUBLANES

    @pl.when(i == 0)
    def _():
        carry_scr[...] = st0_ref[...]

    def gather_u(t, carry):
        t8 = pl.multiple_of(t * SUBLANES, SUBLANES)
        for s in range(B_WIDTH // LANES):
            u_scr[pl.ds(t8, SUBLANES), s * LANES:(s + 1) * LANES] = ub_ref[s, pl.ds(t, SUBLANES, stride=seg), :]
        return carry

    lax.fori_loop(0, seg, gather_u, 0)
    v_scr[...] = jnp.dot(u_scr[...].astype(BF16), wb_ref[...], preferred_element_type=F32)

    lam_r, lam_i = lam_ref[0:1, :], lam_ref[1:2, :]
    lam_r8 = jnp.broadcast_to(lam_r, (SUBLANES, N_STATE))
    lam_i8 = jnp.broadcast_to(lam_i, (SUBLANES, N_STATE))

    def step(t, xr, xi):
        t8 = pl.multiple_of(t * SUBLANES, SUBLANES)
        vr = v_scr[pl.ds(t8, SUBLANES), 0:N_STATE]
        vi = v_scr[pl.ds(t8, SUBLANES), N_STATE:2 * N_STATE]
        return t8, lam_r8 * xr - lam_i8 * xi + vr, lam_r8 * xi + lam_i8 * xr + vi

    def local_pass(t, carry):
        _, nr, ni = step(t, *carry)
        return nr, ni

    zeros = jnp.zeros((SUBLANES, N_STATE), F32)
    end_r, end_i = lax.fori_loop(0, seg, local_pass, (zeros, zeros))

    pw_r, pw_i = lam_r, lam_i
    for _ in range(int(math.log2(seg))):
        pw_r, pw_i = pw_r * pw_r - pw_i * pw_i, 2.0 * pw_r * pw_i
    cr, ci = carry_scr[0:1, :], carry_scr[1:2, :]
    start_r, start_i = [], []
    for j in range(SUBLANES):
        start_r.append(cr)
        start_i.append(ci)
        cr, ci = (end_r[j:j + 1, :] + pw_r * cr - pw_i * ci,
                  end_i[j:j + 1, :] + pw_r * ci + pw_i * cr)
    carry_scr[0:1, :] = cr
    carry_scr[1:2, :] = ci
    st_ref[0:1, :] = cr
    st_ref[1:2, :] = ci

    def full_pass(t, carry):
        t8, nr, ni = step(t, *carry)
        x_scr[pl.ds(t8, SUBLANES), 0:N_STATE] = nr
        x_scr[pl.ds(t8, SUBLANES), N_STATE:2 * N_STATE] = ni
        return nr, ni

    lax.fori_loop(0, seg, full_pass, (jnp.concatenate(start_r, axis=0), jnp.concatenate(start_i, axis=0)))

    y = jnp.dot(x_scr[...].astype(BF16), wc_ref[...], preferred_element_type=F32) + dsk_ref[...] * u_scr[...]
    gl = jax.nn.gelu(y, approximate=True)
    g12 = jnp.dot(gl.astype(BF16), wglu_ref[...], preferred_element_type=F32)
    ob = g12[:, :B_WIDTH] * jax.nn.sigmoid(g12[:, B_WIDTH:])
    u_scr[...] = _rms(ob, gb_ref[:, A_WIDTH:A_WIDTH + B_WIDTH])

    def scatter_nb(t, carry):
        t8 = pl.multiple_of(t * SUBLANES, SUBLANES)
        for s in range(B_WIDTH // LANES):
            nb_scr[s, pl.ds(t, SUBLANES, stride=seg), :] = u_scr[pl.ds(t8, SUBLANES), s * LANES:(s + 1) * LANES]
        return carry

    lax.fori_loop(0, seg, scatter_nb, 0)
    for s in range(B_WIDTH // LANES):
        mix_scr[:, A_WIDTH + s * LANES:A_WIDTH + (s + 1) * LANES] = nb_scr[s].astype(BF16)

    out = jnp.dot(mix_scr[...], wout_ref[...], preferred_element_type=F32)
    o_ref[...] = h_ref[...] + _rms(out, g3_ref[...])


def _mixer(h, qa, ka, va, ka_prev, va_prev, bias, qc, kc, vc, kc_prev, vc_prev, ub, st0,
           sink, lam, wb, wc, dsk, wglu, gb, wout, g3, *, rows, chunk, from_cache):
    b, t, _ = h.shape
    n_ub = B_WIDTH // LANES
    assert t % rows == 0 and rows % chunk == 0 and rows % SUBLANES == 0
    assert (rows // SUBLANES) & (rows // SUBLANES - 1) == 0
    row = lambda w: pl.BlockSpec((None, rows, w), lambda bi, i: (bi, i, 0))
    if from_cache:
        assert t == rows
        prev = lambda n, w: pl.BlockSpec((None, n, w), lambda bi, i: (bi, 0, 0))
    else:
        assert rows % A_BAND == 0
        prev = lambda n, w: pl.BlockSpec((None, n, w), lambda bi, i: (bi, jnp.maximum(i * (rows // n) - 1, 0), 0))
    state = pl.BlockSpec((None, 2, N_STATE), lambda bi, i: (bi, 0, 0))
    kernel = functools.partial(_mixer_kernel, rows=rows, chunk=chunk, masked=not from_cache)
    return pl.pallas_call(
        kernel,
        grid=(b, t // rows),
        in_specs=[pl.BlockSpec(memory_space=pltpu.SMEM),
                  row(D_MODEL), row(A_WIDTH), prev(A_BAND, A_WIDTH), row(A_WIDTH), prev(A_BAND, A_WIDTH), row(A_WIDTH),
                  _resident(bias.shape),
                  row(C_WIDTH), prev(C_BAND, C_KV_WIDTH), row(C_KV_WIDTH), prev(C_BAND, C_KV_WIDTH), row(C_KV_WIDTH),
                  pl.BlockSpec((None, n_ub, rows, LANES), lambda bi, i: (bi, 0, i, 0)),
                  state,
                  _resident((2, N_STATE)), _resident((B_WIDTH, 2 * N_STATE)), _resident((2 * N_STATE, B_WIDTH)),
                  _resident((1, B_WIDTH)), _resident((B_WIDTH, 2 * B_WIDTH)),
                  _resident((1, D_MIX)), _resident((D_MIX, D_MODEL)), _resident((1, D_MODEL))],
        out_specs=[row(D_MODEL), state],
        out_shape=[jax.ShapeDtypeStruct((b, t, D_MODEL), F32),
                   jax.ShapeDtypeStruct((b, 2, N_STATE), F32)],
        scratch_shapes=[pltpu.VMEM((A_BAND + rows, A_WIDTH), BF16),
                        pltpu.VMEM((A_BAND + rows, A_WIDTH), BF16),
                        pltpu.VMEM((C_BAND + rows, C_KV_WIDTH), BF16),
                        pltpu.VMEM((C_BAND + rows, C_KV_WIDTH), BF16),
                        pltpu.VMEM((rows, D_MIX), BF16),
                        pltpu.VMEM((rows, B_WIDTH), F32),
                        pltpu.VMEM((rows, 2 * N_STATE), F32),
                        pltpu.VMEM((rows, 2 * N_STATE), F32),
                        pltpu.VMEM((n_ub, rows, LANES), F32),
                        pltpu.VMEM((2, N_STATE), F32)],
        compiler_params=pltpu.CompilerParams(dimension_semantics=("parallel", "arbitrary"),
                                             vmem_limit_bytes=VMEM_LIMIT_BYTES),
        name="mixer",
    )(sink, h, qa, ka_prev, ka, va_prev, va, bias, qc, kc_prev, kc, vc_prev, vc, ub, st0,
      lam, wb, wc, dsk, wglu, gb, wout, g3)


def _rope_table(pos):
    half = ROPE_DIM // 2
    inv_freq = ROPE_THETA ** (-jnp.arange(half, dtype=F32) / half)
    ang = pos.astype(F32)[:, None] * inv_freq[None, :]
    cos, sin = jnp.cos(ang), jnp.sin(ang)
    t = pos.shape[0]
    ones = jnp.ones((t, HEAD_DIM - ROPE_DIM), F32)
    zeros = jnp.zeros((t, HEAD_DIM - ROPE_DIM), F32)
    zh = jnp.zeros((t, half), F32)
    head = lambda *parts: jnp.concatenate(parts + parts, axis=-1)
    return jnp.stack([head(cos, cos, ones), head(-sin, zh, zeros), head(zh, sin, zeros)])


def _rel_bias(table, chunk):
    rel = A_BAND + jnp.arange(chunk)[:, None] - jnp.arange(A_BAND + chunk)[None, :]
    return table[:, jnp.clip(rel, -REL_CLIP, REL_CLIP) + REL_CLIP]


_C_HEAD_ORDER = (0, 3, 1, 4, 2, 5)


def _permute_c_heads(a, axis):
    shape = a.shape
    a = a.reshape(shape[:axis] + (C_HEADS, HEAD_DIM) + shape[axis + 1:])
    a = jnp.take(a, jnp.array(_C_HEAD_ORDER), axis=axis)
    return a.reshape(shape)


def _layer_params(l, norm_g, branch_norm_g, w_ffn_up, w_ffn_down, w_in, w_out, a_rel_bias, c_sink,
                  ssm_a_re, ssm_a_im, ssm_log_dt, ssm_b_re, ssm_b_im, ssm_c_re, ssm_c_im, ssm_d, w_glu):
    g = norm_g[l]
    win = w_in[l]
    win = jnp.concatenate([win[:, :_O_QC], _permute_c_heads(win[:, _O_QC:_O_KC], 1), win[:, _O_KC:]], axis=1)
    gb = branch_norm_g[l]
    gb = jnp.concatenate([gb[:A_WIDTH + B_WIDTH], _permute_c_heads(gb[A_WIDTH + B_WIDTH:], 0)])
    wout = w_out[l]
    wout = jnp.concatenate([wout[:A_WIDTH + B_WIDTH], _permute_c_heads(wout[A_WIDTH + B_WIDTH:], 0)], axis=0)
    lr, li, bbr, bbi = _ssm_prep(ssm_a_re[l], ssm_a_im[l], ssm_log_dt[l], ssm_b_re[l], ssm_b_im[l])
    as_groups = lambda m: m.reshape(B_GROUPS, B_GROUP_CH, B_STATE)
    wb = jnp.concatenate([_block_diag(as_groups(bbr), B_GROUP_CH, B_STATE),
                          _block_diag(as_groups(bbi), B_GROUP_CH, B_STATE)], axis=1)
    ct = lambda c: jnp.transpose(c, (0, 2, 1))
    wc = jnp.concatenate([_block_diag(ct(ssm_c_re[l]), B_STATE, B_GROUP_CH),
                          -_block_diag(ct(ssm_c_im[l]), B_STATE, B_GROUP_CH)], axis=0)
    return dict(
        g_ffn1=g[0:2], g_in=g[2:3], g_out=g[3:4], g_ffn2=g[4:6],
        wup1=w_ffn_up[l, 0].astype(BF16), wdn1=w_ffn_down[l, 0].astype(BF16),
        wup2=w_ffn_up[l, 1].astype(BF16), wdn2=w_ffn_down[l, 1].astype(BF16),
        win=win.astype(BF16), wout=wout.astype(BF16), gb=gb[None, :],
        table=a_rel_bias[l], sink=c_sink[l].reshape(C_HEADS),
        lam=jnp.stack([lr.reshape(N_STATE), li.reshape(N_STATE)]),
        wb=wb.astype(BF16), wc=wc.astype(BF16), dsk=ssm_d[l].reshape(1, B_WIDTH),
        wglu=w_glu[l].astype(BF16))


def _layer(x, rope_tab, p, cache, *, tm_ffn, tm_proj, rows, chunk):
    b, t, _ = x.shape
    ffn = lambda a, g, wu, wd: _ffn(a.reshape(b * t, D_MODEL), g, wu, wd, tm_ffn).reshape(b, t, D_MODEL)
    h = ffn(x, p['g_ffn1'], p['wup1'], p['wdn1'])
    qa, ka, va, ub, qc, kc, vc = _proj(h, p['g_in'], p['win'], rope_tab, tm_proj)
    bias = _rel_bias(p['table'], chunk)
    if cache is None:
        ka_prev, va_prev, kc_prev, vc_prev = ka, va, kc, vc
        st0 = jnp.zeros((b, 2, N_STATE), F32)
    else:
        a_k, a_v, c_k, c_v, s_re, s_im = cache
        ka_prev, va_prev = a_k.reshape(b, A_BAND, A_WIDTH), a_v.reshape(b, A_BAND, A_WIDTH)
        kc_prev, vc_prev = c_k.reshape(b, C_BAND, C_KV_WIDTH), c_v.reshape(b, C_BAND, C_KV_WIDTH)
        st0 = jnp.stack([s_re.reshape(b, N_STATE), s_im.reshape(b, N_STATE)], axis=1)
    h, st = _mixer(h, qa, ka, va, ka_prev, va_prev, bias, qc, kc, vc, kc_prev, vc_prev, ub, st0,
                   p['sink'], p['lam'], p['wb'], p['wc'], p['dsk'], p['wglu'], p['gb'], p['wout'], p['g_out'],
                   rows=rows, chunk=chunk, from_cache=cache is not None)
    h = ffn(h, p['g_ffn2'], p['wup2'], p['wdn2'])
    if cache is None:
        nak, nav = ka[:, t - A_BAND:], va[:, t - A_BAND:]
        nck, ncv = kc[:, t - C_BAND:], vc[:, t - C_BAND:]
    else:
        nak = jnp.concatenate([ka_prev, ka], axis=1)[:, -A_BAND:]
        nav = jnp.concatenate([va_prev, va], axis=1)[:, -A_BAND:]
        nck = jnp.concatenate([kc_prev, kc], axis=1)[:, -C_BAND:]
        ncv = jnp.concatenate([vc_prev, vc], axis=1)[:, -C_BAND:]
    states = (nak.reshape(b, A_BAND, A_HEADS, HEAD_DIM), nav.reshape(b, A_BAND, A_HEADS, HEAD_DIM),
              nck.reshape(b, C_BAND, C_KV_HEADS, HEAD_DIM), ncv.reshape(b, C_BAND, C_KV_HEADS, HEAD_DIM),
              st[:, 0].reshape(b, B_GROUPS, B_STATE), st[:, 1].reshape(b, B_GROUPS, B_STATE))
    return h, states


def kernel(x_prompt, x_sample, cache_a_k, cache_a_v, cache_c_k, cache_c_v, state_ssm_re, state_ssm_im, norm_g, branch_norm_g, w_ffn_up, w_ffn_down, w_in, w_out, a_rel_bias, c_sink, ssm_a_re, ssm_a_im, ssm_log_dt, ssm_b_re, ssm_b_im, ssm_c_re, ssm_c_im, ssm_d, w_glu):
    depth = norm_g.shape[0]
    t_p, t_s = x_prompt.shape[1], x_sample.shape[1]
    assert t_p >= A_BAND and t_p % A_BAND == 0 and t_s <= CHUNK
    assert cache_a_k.shape[2] == A_BAND and cache_c_k.shape[2] == C_BAND
    rope_p = _rope_table(jnp.arange(t_p))
    rope_s = _rope_table(PAST_LEN + jnp.arange(t_s))
    yp, ys = x_prompt, x_sample
    st_p, st_s = [], []
    for l in range(depth):
        p = _layer_params(l, norm_g, branch_norm_g, w_ffn_up, w_ffn_down, w_in, w_out, a_rel_bias, c_sink,
                          ssm_a_re, ssm_a_im, ssm_log_dt, ssm_b_re, ssm_b_im, ssm_c_re, ssm_c_im, ssm_d, w_glu)
        yp, sp = _layer(yp, rope_p, p, None, tm_ffn=256, tm_proj=512, rows=A_BAND, chunk=CHUNK)
        ys, ss = _layer(ys, rope_s, p, (cache_a_k[l], cache_a_v[l], cache_c_k[l], cache_c_v[l],
                                        state_ssm_re[l], state_ssm_im[l]),
                        tm_ffn=x_sample.shape[0] * t_s, tm_proj=t_s, rows=t_s, chunk=t_s)
        st_p.append(sp)
        st_s.append(ss)
    stack = lambda sts, k: jnp.stack([s[k] for s in sts])
    return (yp, ys) + tuple(stack(st_p, k) for k in range(6)) + tuple(stack(st_s, k) for k in range(6))
```

```python
import functools
import math

import jax
import jax.numpy as jnp
from jax import lax
from jax.experimental import pallas as pl
from jax.experimental.pallas import tpu as pltpu

F32 = jnp.float32
BF16 = jnp.bfloat16

D_MODEL = 1024
HEAD_DIM = 64
CHUNK = 64
A_HEADS = 6
A_WIDTH = A_HEADS * HEAD_DIM
A_BAND = 8 * CHUNK
REL_CLIP = 128
B_GROUPS = 16
B_GROUP_CH = 16
B_WIDTH = B_GROUPS * B_GROUP_CH
B_STATE = 64
N_STATE = B_GROUPS * B_STATE
C_KV_HEADS = 2
C_GROUP = 3
C_HEADS = C_KV_HEADS * C_GROUP
C_WIDTH = C_HEADS * HEAD_DIM
N_SLABS = 3
C_KV_WIDTH = C_KV_HEADS * HEAD_DIM
C_BAND = 128
ROPE_THETA = 500000.0
ROPE_DIM = HEAD_DIM // 4
D_MIX = A_WIDTH + B_WIDTH + C_WIDTH
D_IN = 3 * A_WIDTH + B_WIDTH + C_WIDTH + 2 * C_KV_WIDTH
D_FF = 2816
EPS = 1e-6
NEG_INF = -1e30
ATTN_SCALE = HEAD_DIM ** -0.5
PAST_LEN = 1024

LANES = 128
SUBLANES = 8
VMEM_LIMIT_BYTES = 56 * 1024 * 1024

_O_QA, _O_KA, _O_VA = 0, A_WIDTH, 2 * A_WIDTH
_O_UB = 3 * A_WIDTH
_O_QC = _O_UB + B_WIDTH
_O_KC = _O_QC + C_WIDTH
_O_VC = _O_KC + C_KV_WIDTH


def _rms(x, g):
    return x * lax.rsqrt(jnp.mean(x * x, axis=-1, keepdims=True) + EPS) * g


def _resident(shape):
    nd = len(shape)
    return pl.BlockSpec(shape, lambda *_: (0,) * nd, pipeline_mode=pl.Buffered(1))


def _ffn_kernel(x_ref, g_ref, wup_ref, wdn_ref, o_ref):
    x = x_ref[...]
    xn = _rms(x, g_ref[0:1, :]).astype(BF16)
    mid = jnp.dot(xn, wup_ref[...], preferred_element_type=F32)
    gate, up = mid[:, :D_FF], mid[:, D_FF:]
    act = (gate * jax.nn.sigmoid(gate) * up).astype(BF16)
    y = jnp.dot(act, wdn_ref[...], preferred_element_type=F32)
    o_ref[...] = x + 0.5 * _rms(y, g_ref[1:2, :])


def _ffn(x2d, g2, wup, wdn, tm):
    n = x2d.shape[0]
    return pl.pallas_call(
        _ffn_kernel,
        grid=(n // tm,),
        in_specs=[pl.BlockSpec((tm, D_MODEL), lambda i: (i, 0)),
                  _resident((2, D_MODEL)),
                  _resident((D_MODEL, 2 * D_FF)),
                  _resident((D_FF, D_MODEL))],
        out_specs=pl.BlockSpec((tm, D_MODEL), lambda i: (i, 0)),
        out_shape=jax.ShapeDtypeStruct((n, D_MODEL), F32),
        compiler_params=pltpu.CompilerParams(dimension_semantics=("parallel",),
                                             vmem_limit_bytes=VMEM_LIMIT_BYTES),
        name="ffn",
    )(x2d, g2, wup, wdn)


def _rope_slab(x, cos, sin_hi, sin_lo):
    half = ROPE_DIM // 2
    return (x * cos + pltpu.roll(x, LANES - half, 1) * sin_hi + pltpu.roll(x, half, 1) * sin_lo)


def _proj_kernel(h_ref, g_ref, win_ref, rope_ref,
                 qa_ref, ka_ref, va_ref, ub_ref, qc_ref, kc_ref, vc_ref):
    hn = _rms(h_ref[...], g_ref[...]).astype(BF16)
    p = jnp.dot(hn, win_ref[...], preferred_element_type=F32)
    cos, sin_hi, sin_lo = rope_ref[0], rope_ref[1], rope_ref[2]
    qa_ref[...] = (p[:, _O_QA:_O_QA + A_WIDTH] * ATTN_SCALE).astype(BF16)
    ka_ref[...] = p[:, _O_KA:_O_KA + A_WIDTH]
    va_ref[...] = p[:, _O_VA:_O_VA + A_WIDTH]
    for s in range(B_WIDTH // LANES):
        ub_ref[s] = p[:, _O_UB + s * LANES:_O_UB + (s + 1) * LANES]
    for s in range(C_WIDTH // LANES):
        q = p[:, _O_QC + s * LANES:_O_QC + (s + 1) * LANES]
        qc_ref[:, s * LANES:(s + 1) * LANES] = (_rope_slab(q, cos, sin_hi, sin_lo) * ATTN_SCALE).astype(BF16)
    kc_ref[...] = _rope_slab(p[:, _O_KC:_O_KC + C_KV_WIDTH], cos, sin_hi, sin_lo)
    vc_ref[...] = p[:, _O_VC:_O_VC + C_KV_WIDTH]


def _proj(h, g, win, rope_tab, tm):
    b, t, _ = h.shape
    row = lambda w: pl.BlockSpec((None, tm, w), lambda bi, i: (bi, i, 0))
    n_ub = B_WIDTH // LANES
    return pl.pallas_call(
        _proj_kernel,
        grid=(b, t // tm),
        in_specs=[row(D_MODEL),
                  _resident((1, D_MODEL)),
                  _resident((D_MODEL, D_IN)),
                  pl.BlockSpec((3, tm, LANES), lambda bi, i: (0, i, 0))],
        out_specs=[row(A_WIDTH), row(A_WIDTH), row(A_WIDTH),
                   pl.BlockSpec((None, n_ub, tm, LANES), lambda bi, i: (bi, 0, i, 0)),
                   row(C_WIDTH), row(C_KV_WIDTH), row(C_KV_WIDTH)],
        out_shape=[jax.ShapeDtypeStruct((b, t, A_WIDTH), BF16),
                   jax.ShapeDtypeStruct((b, t, A_WIDTH), F32),
                   jax.ShapeDtypeStruct((b, t, A_WIDTH), F32),
                   jax.ShapeDtypeStruct((b, n_ub, t, LANES), F32),
                   jax.ShapeDtypeStruct((b, t, C_WIDTH), BF16),
                   jax.ShapeDtypeStruct((b, t, C_KV_WIDTH), F32),
                   jax.ShapeDtypeStruct((b, t, C_KV_WIDTH), F32)],
        compiler_params=pltpu.CompilerParams(dimension_semantics=("parallel", "parallel"),
                                             vmem_limit_bytes=VMEM_LIMIT_BYTES),
        name="proj",
    )(h, g, win, rope_tab)


def _ssm_prep_kernel(ar_ref, ai_ref, ldt_ref, br_ref, bi_ref, lr_ref, li_ref, bbr_ref, bbi_ref):
    ar, ai = ar_ref[...], ai_ref[...]
    dt = jnp.exp(ldt_ref[...])
    mag = jnp.exp(ar * dt)
    lr, li = mag * jnp.cos(ai * dt), mag * jnp.sin(ai * dt)
    den = ar * ar + ai * ai
    zr = ((lr - 1.0) * ar + li * ai) / den
    zi = (li * ar - (lr - 1.0) * ai) / den
    br, bi = br_ref[...], bi_ref[...]
    lr_ref[...] = lr
    li_ref[...] = li
    bbr_ref[...] = zr * br - zi * bi
    bbi_ref[...] = zr * bi + zi * br


def _ssm_prep(a_re, a_im, log_dt, b_re, b_im):
    rep = lambda a: jnp.repeat(a, B_GROUP_CH, axis=0)
    rows = B_GROUPS * B_GROUP_CH
    bt = lambda b: jnp.transpose(b, (0, 2, 1)).reshape(rows, B_STATE)
    sds = jax.ShapeDtypeStruct((rows, B_STATE), F32)
    lr, li, bbr, bbi = pl.pallas_call(
        _ssm_prep_kernel, out_shape=[sds, sds, sds, sds], name="ssm_prep",
    )(rep(a_re), rep(a_im), rep(log_dt[:, None]), bt(b_re), bt(b_im))
    return lr[::B_GROUP_CH], li[::B_GROUP_CH], bbr, bbi


def _block_diag(m, rows_per_group, cols_per_group):
    eye = jnp.eye(B_GROUPS, dtype=m.dtype)
    full = m[:, :, None, :] * eye[:, None, :, None]
    return full.reshape(B_GROUPS * rows_per_group, B_GROUPS * cols_per_group)


def _mixer_kernel(sink_ref,
                  h_ref, qa_ref, kap_ref, kac_ref, vap_ref, vac_ref, bias_ref,
                  qc_ref, kcp_ref, kcc_ref, vcp_ref, vcc_ref,
                  ub_ref, st0_ref, lam_ref, wb_ref, wc_ref, dsk_ref, wglu_ref,
                  gb_ref, wout_ref, g3_ref,
                  o_ref, st_ref,
                  kbuf, vbuf, kcbuf, vcbuf, sa_scr, sc_scr, pa_scr, pc_scr, att_scr,
                  mix_scr, u_scr, v_scr, x_scr, nb_scr, carry_scr,
                  *, rows, chunk, masked):
    i = pl.program_id(1)
    n_chunks = rows // chunk
    band_a = A_BAND + chunk
    band_c = C_BAND + chunk

    kbuf[0:A_BAND, :] = kap_ref[...].astype(BF16)
    kbuf[A_BAND:A_BAND + rows, :] = kac_ref[...].astype(BF16)
    for s in range(N_SLABS):
        vbuf[0:A_BAND, 2 * s * LANES:(2 * s + 1) * LANES] = vap_ref[:, s * LANES:(s + 1) * LANES].astype(BF16)
        vbuf[A_BAND:A_BAND + rows, 2 * s * LANES:(2 * s + 1) * LANES] = (
            vac_ref[:, s * LANES:(s + 1) * LANES].astype(BF16))
        vbuf[:, (2 * s + 1) * LANES:(2 * s + 2) * LANES] = jnp.ones((A_BAND + rows, LANES), BF16)
    kcbuf[0:C_BAND, :] = kcp_ref[...].astype(BF16)
    kcbuf[C_BAND:C_BAND + rows, :] = kcc_ref[...].astype(BF16)
    vcbuf[0:C_BAND, :] = vcp_ref[...].astype(BF16)
    vcbuf[C_BAND:C_BAND + rows, :] = vcc_ref[...].astype(BF16)

    lane = lax.broadcasted_iota(jnp.int32, (1, LANES), 1)
    first_head = lane < HEAD_DIM
    contract_last = (((1,), (1,)), ((), ()))

    contract_first = (((0,), (0,)), ((), ()))
    first_half = lax.broadcasted_iota(jnp.int32, (1, 2 * chunk), 1) < chunk

    def score_chunk(c, slot, mask_early):
        r0 = pl.multiple_of(c * chunk, chunk)
        q_pos = i * rows + c * chunk

        def pair_scores(q_ref, keys, s, back, bias):
            q = q_ref[pl.ds(r0, chunk), s * LANES:(s + 1) * LANES]
            zero = jnp.zeros_like(q)
            both = jnp.concatenate([jnp.where(first_head, q, zero), jnp.where(first_head, zero, q)], axis=0)
            st = lax.dot_general(keys, both, contract_last, preferred_element_type=F32)
            if bias is not None:
                st = st + bias
            if mask_early:
                st = jnp.where(lax.broadcasted_iota(jnp.int32, st.shape, 0) >= back - q_pos, st, NEG_INF)
            return st

        for s in range(N_SLABS):
            sa_scr[slot, s] = pair_scores(qa_ref, kbuf[pl.ds(r0, band_a), s * LANES:(s + 1) * LANES], s, A_BAND,
                                          bias_ref[s])
        keys = kcbuf[pl.ds(r0, band_c), :]
        for s in range(N_SLABS):
            sc_scr[slot, s] = pair_scores(qc_ref, keys, s, C_BAND, None)

    def softmax_chunk(slot):
        for s in range(N_SLABS):
            m = jnp.max(sa_scr[slot, s], axis=0, keepdims=True)
            pa_scr[slot, s] = jnp.exp(sa_scr[slot, s] - m).astype(BF16)
        for s in range(N_SLABS):
            st = sc_scr[slot, s]
            sink = jnp.where(first_half, sink_ref[s], sink_ref[C_GROUP + s])
            m = jnp.maximum(jnp.max(st, axis=0, keepdims=True), sink)
            e = jnp.exp(st - m)
            den = jnp.sum(e, axis=0, keepdims=True) + jnp.exp(sink - m)
            pc_scr[slot, s] = (e * (1.0 / den)).astype(BF16)

    def value_chunk(c, slot):
        r0 = pl.multiple_of(c * chunk, chunk)

        def merge_heads(o):
            return jnp.where(first_head, o[:chunk], o[chunk:])

        for s in range(N_SLABS):
            o2 = lax.dot_general(pa_scr[slot, s], vbuf[pl.ds(r0, band_a), 2 * s * LANES:(2 * s + 2) * LANES],
                                 contract_first, preferred_element_type=F32)
            att_scr[pl.ds(r0, chunk), s * LANES:(s + 1) * LANES] = merge_heads(o2[:, :LANES] * (1.0 / o2[:, LANES:]))
        vals = vcbuf[pl.ds(r0, band_c), :]
        for s in range(N_SLABS):
            o = lax.dot_general(pc_scr[slot, s], vals, contract_first, preferred_element_type=F32)
            att_scr[pl.ds(r0, chunk), A_WIDTH + s * LANES:A_WIDTH + (s + 1) * LANES] = merge_heads(o)

    def run_chunks(mask_early):
        score_chunk(0, 0, mask_early)
        if n_chunks == 1:
            softmax_chunk(0)
            value_chunk(0, 0)
            return

        def pair(g, carry):
            c0 = 2 * g
            score_chunk(c0 + 1, 1, mask_early)
            softmax_chunk(0)
            value_chunk(c0, 0)
            score_chunk(jnp.minimum(c0 + 2, n_chunks - 1), 0, mask_early)
            softmax_chunk(1)
            value_chunk(c0 + 1, 1)
            return carry

        lax.fori_loop(0, n_chunks // 2, pair, 0)

    if masked:
        pl.when(i == 0)(lambda: run_chunks(True))
        pl.when(i != 0)(lambda: run_chunks(False))
    else:
        run_chunks(False)
    mix_scr[:, 0:A_WIDTH] = _rms(att_scr[:, 0:A_WIDTH], gb_ref[:, 0:A_WIDTH]).astype(BF16)
    mix_scr[:, A_WIDTH + B_WIDTH:D_MIX] = _rms(att_scr[:, A_WIDTH:A_WIDTH + C_WIDTH],
                                               gb_ref[:, A_WIDTH + B_WIDTH:D_MIX]).astype(BF16)

    seg = rows // SUBLANES

    @pl.when(i == 0)
    def _():
        carry_scr[...] = st0_ref[...]

    def gather_u(t, carry):
        t8 = pl.multiple_of(t * SUBLANES, SUBLANES)
        for s in range(B_WIDTH // LANES):
            u_scr[pl.ds(t8, SUBLANES), s * LANES:(s + 1) * LANES] = ub_ref[s, pl.ds(t, SUBLANES, stride=seg), :]
        return carry

    lax.fori_loop(0, seg, gather_u, 0)
    v_scr[...] = jnp.dot(u_scr[...].astype(BF16), wb_ref[...], preferred_element_type=F32)

    lam_r, lam_i = lam_ref[0:1, :], lam_ref[1:2, :]
    lam_r8 = jnp.broadcast_to(lam_r, (SUBLANES, N_STATE))
    lam_i8 = jnp.broadcast_to(lam_i, (SUBLANES, N_STATE))

    def step(t, xr, xi):
        t8 = pl.multiple_of(t * SUBLANES, SUBLANES)
        vr = v_scr[pl.ds(t8, SUBLANES), 0:N_STATE]
        vi = v_scr[pl.ds(t8, SUBLANES), N_STATE:2 * N_STATE]
        return t8, lam_r8 * xr - lam_i8 * xi + vr, lam_r8 * xi + lam_i8 * xr + vi

    def local_pass(t, carry):
        _, nr, ni = step(t, *carry)
        return nr, ni

    zeros = jnp.zeros((SUBLANES, N_STATE), F32)
    end_r, end_i = lax.fori_loop(0, seg, local_pass, (zeros, zeros))

    pw_r, pw_i = lam_r, lam_i
    for _ in range(int(math.log2(seg))):
        pw_r, pw_i = pw_r * pw_r - pw_i * pw_i, 2.0 * pw_r * pw_i
    cr, ci = carry_scr[0:1, :], carry_scr[1:2, :]
    start_r, start_i = [], []
    for j in range(SUBLANES):
        start_r.append(cr)
        start_i.append(ci)
        cr, ci = (end_r[j:j + 1, :] + pw_r * cr - pw_i * ci,
                  end_i[j:j + 1, :] + pw_r * ci + pw_i * cr)
    carry_scr[0:1, :] = cr
    carry_scr[1:2, :] = ci
    st_ref[0:1, :] = cr
    st_ref[1:2, :] = ci

    def full_pass(t, carry):
        t8, nr, ni = step(t, *carry)
        x_scr[pl.ds(t8, SUBLANES), 0:N_STATE] = nr
        x_scr[pl.ds(t8, SUBLANES), N_STATE:2 * N_STATE] = ni
        return nr, ni

    lax.fori_loop(0, seg, full_pass, (jnp.concatenate(start_r, axis=0), jnp.concatenate(start_i, axis=0)))

    y = jnp.dot(x_scr[...].astype(BF16), wc_ref[...], preferred_element_type=F32) + dsk_ref[...] * u_scr[...]
    gl = jax.nn.gelu(y, approximate=True)
    g12 = jnp.dot(gl.astype(BF16), wglu_ref[...], preferred_element_type=F32)
    ob = g12[:, :B_WIDTH] * jax.nn.sigmoid(g12[:, B_WIDTH:])
    u_scr[...] = _rms(ob, gb_ref[:, A_WIDTH:A_WIDTH + B_WIDTH])

    def scatter_nb(t, carry):
        t8 = pl.multiple_of(t * SUBLANES, SUBLANES)
        for s in range(B_WIDTH // LANES):
            nb_scr[s, pl.ds(t, SUBLANES, stride=seg), :] = u_scr[pl.ds(t8, SUBLANES), s * LANES:(s + 1) * LANES]
        return carry

    lax.fori_loop(0, seg, scatter_nb, 0)
    for s in range(B_WIDTH // LANES):
        mix_scr[:, A_WIDTH + s * LANES:A_WIDTH + (s + 1) * LANES] = nb_scr[s].astype(BF16)

    out = jnp.dot(mix_scr[...], wout_ref[...], preferred_element_type=F32)
    o_ref[...] = h_ref[...] + _rms(out, g3_ref[...])


def _mixer(h, qa, ka, va, ka_prev, va_prev, bias, qc, kc, vc, kc_prev, vc_prev, ub, st0,
           sink, lam, wb, wc, dsk, wglu, gb, wout, g3, *, rows, chunk, from_cache):
    b, t, _ = h.shape
    n_ub = B_WIDTH // LANES
    assert t % rows == 0 and rows % chunk == 0 and rows % SUBLANES == 0
    assert (rows // SUBLANES) & (rows // SUBLANES - 1) == 0
    row = lambda w: pl.BlockSpec((None, rows, w), lambda bi, i: (bi, i, 0))
    if from_cache:
        assert t == rows
        prev = lambda n, w: pl.BlockSpec((None, n, w), lambda bi, i: (bi, 0, 0))
    else:
        assert rows % A_BAND == 0
        prev = lambda n, w: pl.BlockSpec((None, n, w), lambda bi, i: (bi, jnp.maximum(i * (rows // n) - 1, 0), 0))
    state = pl.BlockSpec((None, 2, N_STATE), lambda bi, i: (bi, 0, 0))
    n_chunks = rows // chunk
    assert n_chunks == 1 or n_chunks % 2 == 0
    slots = min(2, n_chunks)
    kernel = functools.partial(_mixer_kernel, rows=rows, chunk=chunk, masked=not from_cache)
    return pl.pallas_call(
        kernel,
        grid=(b, t // rows),
        in_specs=[pl.BlockSpec(memory_space=pltpu.SMEM),
                  row(D_MODEL), row(A_WIDTH), prev(A_BAND, A_WIDTH), row(A_WIDTH), prev(A_BAND, A_WIDTH), row(A_WIDTH),
                  _resident(bias.shape),
                  row(C_WIDTH), prev(C_BAND, C_KV_WIDTH), row(C_KV_WIDTH), prev(C_BAND, C_KV_WIDTH), row(C_KV_WIDTH),
                  pl.BlockSpec((None, n_ub, rows, LANES), lambda bi, i: (bi, 0, i, 0)),
                  state,
                  _resident((2, N_STATE)), _resident((B_WIDTH, 2 * N_STATE)), _resident((2 * N_STATE, B_WIDTH)),
                  _resident((1, B_WIDTH)), _resident((B_WIDTH, 2 * B_WIDTH)),
                  _resident((1, D_MIX)), _resident((D_MIX, D_MODEL)), _resident((1, D_MODEL))],
        out_specs=[row(D_MODEL), state],
        out_shape=[jax.ShapeDtypeStruct((b, t, D_MODEL), F32),
                   jax.ShapeDtypeStruct((b, 2, N_STATE), F32)],
        scratch_shapes=[pltpu.VMEM((A_BAND + rows, A_WIDTH), BF16),
                        pltpu.VMEM((A_BAND + rows, 2 * A_WIDTH), BF16),
                        pltpu.VMEM((C_BAND + rows, C_KV_WIDTH), BF16),
                        pltpu.VMEM((C_BAND + rows, C_KV_WIDTH), BF16),
                        pltpu.VMEM((slots, N_SLABS, A_BAND + chunk, 2 * chunk), F32),
                        pltpu.VMEM((slots, N_SLABS, C_BAND + chunk, 2 * chunk), F32),
                        pltpu.VMEM((slots, N_SLABS, A_BAND + chunk, 2 * chunk), BF16),
                        pltpu.VMEM((slots, N_SLABS, C_BAND + chunk, 2 * chunk), BF16),
                        pltpu.VMEM((rows, A_WIDTH + C_WIDTH), F32),
                        pltpu.VMEM((rows, D_MIX), BF16),
                        pltpu.VMEM((rows, B_WIDTH), F32),
                        pltpu.VMEM((rows, 2 * N_STATE), F32),
                        pltpu.VMEM((rows, 2 * N_STATE), F32),
                        pltpu.VMEM((n_ub, rows, LANES), F32),
                        pltpu.VMEM((2, N_STATE), F32)],
        compiler_params=pltpu.CompilerParams(dimension_semantics=("parallel", "arbitrary"),
                                             vmem_limit_bytes=VMEM_LIMIT_BYTES),
        name="mixer",
    )(sink, h, qa, ka_prev, ka, va_prev, va, bias, qc, kc_prev, kc, vc_prev, vc, ub, st0,
      lam, wb, wc, dsk, wglu, gb, wout, g3)


def _rope_table(pos):
    half = ROPE_DIM // 2
    inv_freq = ROPE_THETA ** (-jnp.arange(half, dtype=F32) / half)
    ang = pos.astype(F32)[:, None] * inv_freq[None, :]
    cos, sin = jnp.cos(ang), jnp.sin(ang)
    t = pos.shape[0]
    ones = jnp.ones((t, HEAD_DIM - ROPE_DIM), F32)
    zeros = jnp.zeros((t, HEAD_DIM - ROPE_DIM), F32)
    zh = jnp.zeros((t, half), F32)
    head = lambda *parts: jnp.concatenate(parts + parts, axis=-1)
    return jnp.stack([head(cos, cos, ones), head(-sin, zh, zeros), head(zh, sin, zeros)])


def _rel_bias(table, chunk):
    n_heads = table.shape[0]
    period = A_BAND + 2 * CHUNK
    top = table[:, 2 * REL_CLIP:]
    ramp = jnp.flip(table[:, REL_CLIP - CHUNK + 1:2 * REL_CLIP], axis=1)
    n_far = A_BAND - REL_CLIP + 1
    row = jnp.concatenate([jnp.broadcast_to(top, (n_heads, n_far)), ramp,
                           jnp.broadcast_to(top, (n_heads, period - n_far - ramp.shape[1]))], axis=1)
    rolled = jnp.tile(row, (1, CHUNK))[:, :CHUNK * (period - 1)].reshape(n_heads, CHUNK, period - 1)
    return rolled[:, :chunk, :A_BAND + chunk]


def _rel_bias_pairs(table, chunk):
    band = A_BAND + chunk
    bias = _rel_bias(table, chunk).reshape(N_SLABS, 2, chunk, band)
    return jnp.transpose(bias, (0, 3, 1, 2)).reshape(N_SLABS, band, 2 * chunk)


def _permute_c_heads(a, axis):
    shape = a.shape
    a = a.reshape(shape[:axis] + (C_KV_HEADS, C_GROUP, HEAD_DIM) + shape[axis + 1:])
    return jnp.swapaxes(a, axis, axis + 1).reshape(shape)


def _layer_params(l, norm_g, branch_norm_g, w_ffn_up, w_ffn_down, w_in, w_out, a_rel_bias, c_sink,
                  ssm_a_re, ssm_a_im, ssm_log_dt, ssm_b_re, ssm_b_im, ssm_c_re, ssm_c_im, ssm_d, w_glu):
    g = norm_g[l]
    win = w_in[l]
    win = jnp.concatenate([win[:, :_O_QC], _permute_c_heads(win[:, _O_QC:_O_KC], 1), win[:, _O_KC:]], axis=1)
    gb = branch_norm_g[l]
    gb = jnp.concatenate([gb[:A_WIDTH + B_WIDTH], _permute_c_heads(gb[A_WIDTH + B_WIDTH:], 0)])
    wout = w_out[l]
    wout = jnp.concatenate([wout[:A_WIDTH + B_WIDTH], _permute_c_heads(wout[A_WIDTH + B_WIDTH:], 0)], axis=0)
    lr, li, bbr, bbi = _ssm_prep(ssm_a_re[l], ssm_a_im[l], ssm_log_dt[l], ssm_b_re[l], ssm_b_im[l])
    as_groups = lambda m: m.reshape(B_GROUPS, B_GROUP_CH, B_STATE)
    wb = jnp.concatenate([_block_diag(as_groups(bbr), B_GROUP_CH, B_STATE),
                          _block_diag(as_groups(bbi), B_GROUP_CH, B_STATE)], axis=1)
    ct = lambda c: jnp.transpose(c, (0, 2, 1))
    wc = jnp.concatenate([_block_diag(ct(ssm_c_re[l]), B_STATE, B_GROUP_CH),
                          -_block_diag(ct(ssm_c_im[l]), B_STATE, B_GROUP_CH)], axis=0)
    return dict(
        g_ffn1=g[0:2], g_in=g[2:3], g_out=g[3:4], g_ffn2=g[4:6],
        wup1=w_ffn_up[l, 0].astype(BF16), wdn1=w_ffn_down[l, 0].astype(BF16),
        wup2=w_ffn_up[l, 1].astype(BF16), wdn2=w_ffn_down[l, 1].astype(BF16),
        win=win.astype(BF16), wout=wout.astype(BF16), gb=gb[None, :],
        table=a_rel_bias[l], sink=c_sink[l].reshape(C_HEADS),
        lam=jnp.stack([lr.reshape(N_STATE), li.reshape(N_STATE)]),
        wb=wb.astype(BF16), wc=wc.astype(BF16), dsk=ssm_d[l].reshape(1, B_WIDTH),
        wglu=w_glu[l].astype(BF16))


def _layer(x, rope_tab, p, cache, *, tm_ffn, tm_proj, rows, chunk):
    b, t, _ = x.shape
    ffn = lambda a, g, wu, wd: _ffn(a.reshape(b * t, D_MODEL), g, wu, wd, tm_ffn).reshape(b, t, D_MODEL)
    h = ffn(x, p['g_ffn1'], p['wup1'], p['wdn1'])
    qa, ka, va, ub, qc, kc, vc = _proj(h, p['g_in'], p['win'], rope_tab, tm_proj)
    bias = _rel_bias_pairs(p['table'], chunk)
    if cache is None:
        ka_prev, va_prev, kc_prev, vc_prev = ka, va, kc, vc
        st0 = jnp.zeros((b, 2, N_STATE), F32)
    else:
        a_k, a_v, c_k, c_v, s_re, s_im = cache
        ka_prev, va_prev = a_k.reshape(b, A_BAND, A_WIDTH), a_v.reshape(b, A_BAND, A_WIDTH)
        kc_prev, vc_prev = c_k.reshape(b, C_BAND, C_KV_WIDTH), c_v.reshape(b, C_BAND, C_KV_WIDTH)
        st0 = jnp.stack([s_re.reshape(b, N_STATE), s_im.reshape(b, N_STATE)], axis=1)
    h, st = _mixer(h, qa, ka, va, ka_prev, va_prev, bias, qc, kc, vc, kc_prev, vc_prev, ub, st0,
                   p['sink'], p['lam'], p['wb'], p['wc'], p['dsk'], p['wglu'], p['gb'], p['wout'], p['g_out'],
                   rows=rows, chunk=chunk, from_cache=cache is not None)
    h = ffn(h, p['g_ffn2'], p['wup2'], p['wdn2'])
    if cache is None:
        nak, nav = ka[:, t - A_BAND:], va[:, t - A_BAND:]
        nck, ncv = kc[:, t - C_BAND:], vc[:, t - C_BAND:]
    else:
        nak = jnp.concatenate([ka_prev, ka], axis=1)[:, -A_BAND:]
        nav = jnp.concatenate([va_prev, va], axis=1)[:, -A_BAND:]
        nck = jnp.concatenate([kc_prev, kc], axis=1)[:, -C_BAND:]
        ncv = jnp.concatenate([vc_prev, vc], axis=1)[:, -C_BAND:]
    states = (nak.reshape(b, A_BAND, A_HEADS, HEAD_DIM), nav.reshape(b, A_BAND, A_HEADS, HEAD_DIM),
              nck.reshape(b, C_BAND, C_KV_HEADS, HEAD_DIM), ncv.reshape(b, C_BAND, C_KV_HEADS, HEAD_DIM),
              st[:, 0].reshape(b, B_GROUPS, B_STATE), st[:, 1].reshape(b, B_GROUPS, B_STATE))
    return h, states


def kernel(x_prompt, x_sample, cache_a_k, cache_a_v, cache_c_k, cache_c_v, state_ssm_re, state_ssm_im, norm_g, branch_norm_g, w_ffn_up, w_ffn_down, w_in, w_out, a_rel_bias, c_sink, ssm_a_re, ssm_a_im, ssm_log_dt, ssm_b_re, ssm_b_im, ssm_c_re, ssm_c_im, ssm_d, w_glu):
    depth = norm_g.shape[0]
    t_p, t_s = x_prompt.shape[1], x_sample.shape[1]
    assert t_p >= A_BAND and t_p % A_BAND == 0 and t_s <= CHUNK
    assert cache_a_k.shape[2] == A_BAND and cache_c_k.shape[2] == C_BAND
    rope_p = _rope_table(jnp.arange(t_p))
    rope_s = _rope_table(PAST_LEN + jnp.arange(t_s))
    yp, ys = x_prompt, x_sample
    st_p, st_s = [], []
    for l in range(depth):
        p = _layer_params(l, norm_g, branch_norm_g, w_ffn_up, w_ffn_down, w_in, w_out, a_rel_bias, c_sink,
                          ssm_a_re, ssm_a_im, ssm_log_dt, ssm_b_re, ssm_b_im, ssm_c_re, ssm_c_im, ssm_d, w_glu)
        yp, sp = _layer(yp, rope_p, p, None, tm_ffn=512, tm_proj=512, rows=A_BAND, chunk=CHUNK)
        ys, ss = _layer(ys, rope_s, p, (cache_a_k[l], cache_a_v[l], cache_c_k[l], cache_c_v[l],
                                        state_ssm_re[l], state_ssm_im[l]),
                        tm_ffn=x_sample.shape[0] * t_s, tm_proj=t_s, rows=t_s, chunk=t_s)
        st_p.append(sp)
        st_s.append(ss)
    stack = lambda sts, k: jnp.stack([s[k] for s in sts])
    return (yp, ys) + tuple(stack(st_p, k) for k in range(6)) + tuple(stack(st_s, k) for k in range(6))
```

```python
import functools
import math

import jax
import jax.numpy as jnp
from jax import lax
from jax.experimental import pallas as pl
from jax.experimental.pallas import tpu as pltpu

F32 = jnp.float32
BF16 = jnp.bfloat16

D_MODEL = 1024
HEAD_DIM = 64
CHUNK = 64
A_HEADS = 6
A_WIDTH = A_HEADS * HEAD_DIM
A_BAND = 8 * CHUNK
REL_CLIP = 128
B_GROUPS = 16
B_GROUP_CH = 16
B_WIDTH = B_GROUPS * B_GROUP_CH
B_STATE = 64
N_STATE = B_GROUPS * B_STATE
C_KV_HEADS = 2
C_GROUP = 3
C_HEADS = C_KV_HEADS * C_GROUP
C_WIDTH = C_HEADS * HEAD_DIM
N_SLABS = 3
SCAN_UNROLL = 2
MOVE_UNROLL = 4
C_KV_WIDTH = C_KV_HEADS * HEAD_DIM
C_BAND = 128
ROPE_THETA = 500000.0
ROPE_DIM = HEAD_DIM // 4
D_MIX = A_WIDTH + B_WIDTH + C_WIDTH
D_IN = 3 * A_WIDTH + B_WIDTH + C_WIDTH + 2 * C_KV_WIDTH
D_FF = 2816
EPS = 1e-6
NEG_INF = -1e30
ATTN_SCALE = HEAD_DIM ** -0.5
PAST_LEN = 1024

LANES = 128
SUBLANES = 8
VMEM_LIMIT_BYTES = 56 * 1024 * 1024

_O_QA, _O_KA, _O_VA = 0, A_WIDTH, 2 * A_WIDTH
_O_UB = 3 * A_WIDTH
_O_QC = _O_UB + B_WIDTH
_O_KC = _O_QC + C_WIDTH
_O_VC = _O_KC + C_KV_WIDTH


def _rms(x, g):
    return x * lax.rsqrt(jnp.mean(x * x, axis=-1, keepdims=True) + EPS) * g


def _param_block(block_shape, index):
    assert len(block_shape) == len(index)
    return pl.BlockSpec(block_shape, lambda *_: index, pipeline_mode=pl.Buffered(1))


def _ffn_kernel(x_ref, g_pre_ref, g_post_ref, wup_ref, wdn_ref, o_ref):
    x = x_ref[...]
    xn = _rms(x, g_pre_ref[...]).astype(BF16)
    mid = jnp.dot(xn, wup_ref[...], preferred_element_type=F32)
    gate, up = mid[:, :D_FF], mid[:, D_FF:]
    act = (gate * jax.nn.sigmoid(gate) * up).astype(BF16)
    y = jnp.dot(act, wdn_ref[...], preferred_element_type=F32)
    o_ref[...] = x + 0.5 * _rms(y, g_post_ref[...])


def _ffn(x2d, norm_g, wup, wdn, layer, which, tm):
    n = x2d.shape[0]
    gain = lambda k: _param_block((None, None, 1, D_MODEL), (layer, k, 0, 0))
    return pl.pallas_call(
        _ffn_kernel,
        grid=(n // tm,),
        in_specs=[pl.BlockSpec((tm, D_MODEL), lambda i: (i, 0)),
                  gain(4 * which), gain(4 * which + 1),
                  _param_block((None, None, D_MODEL, 2 * D_FF), (layer, which, 0, 0)),
                  _param_block((None, None, D_FF, D_MODEL), (layer, which, 0, 0))],
        out_specs=pl.BlockSpec((tm, D_MODEL), lambda i: (i, 0)),
        out_shape=jax.ShapeDtypeStruct((n, D_MODEL), F32),
        compiler_params=pltpu.CompilerParams(dimension_semantics=("parallel",),
                                             vmem_limit_bytes=VMEM_LIMIT_BYTES),
        name="ffn",
    )(x2d, norm_g, norm_g, wup, wdn)


def _rope_slab(x, cos, sin_hi, sin_lo):
    half = ROPE_DIM // 2
    return (x * cos + pltpu.roll(x, LANES - half, 1) * sin_hi + pltpu.roll(x, half, 1) * sin_lo)


def _proj_kernel(h_ref, g_ref, win_ref, rope_ref,
                 qa_ref, ka_ref, va_ref, ub_ref, qc_ref, kc_ref, vc_ref):
    hn = _rms(h_ref[...], g_ref[...]).astype(BF16)
    p = jnp.dot(hn, win_ref[...], preferred_element_type=F32)
    cos, sin_hi, sin_lo = rope_ref[0], rope_ref[1], rope_ref[2]
    qa_ref[...] = (p[:, _O_QA:_O_QA + A_WIDTH] * ATTN_SCALE).astype(BF16)
    ka_ref[...] = p[:, _O_KA:_O_KA + A_WIDTH]
    va_ref[...] = p[:, _O_VA:_O_VA + A_WIDTH]
    for s in range(B_WIDTH // LANES):
        ub_ref[s] = p[:, _O_UB + s * LANES:_O_UB + (s + 1) * LANES]
    for s in range(C_WIDTH // LANES):
        q = p[:, _O_QC + s * LANES:_O_QC + (s + 1) * LANES]
        qc_ref[:, s * LANES:(s + 1) * LANES] = (_rope_slab(q, cos, sin_hi, sin_lo) * ATTN_SCALE).astype(BF16)
    kc_ref[...] = _rope_slab(p[:, _O_KC:_O_KC + C_KV_WIDTH], cos, sin_hi, sin_lo)
    vc_ref[...] = p[:, _O_VC:_O_VC + C_KV_WIDTH]


def _proj(h, norm_g, win, rope_tab, layer, tm):
    b, t, _ = h.shape
    row = lambda w: pl.BlockSpec((None, tm, w), lambda bi, i: (bi, i, 0))
    n_ub = B_WIDTH // LANES
    return pl.pallas_call(
        _proj_kernel,
        grid=(b, t // tm),
        in_specs=[row(D_MODEL),
                  _param_block((None, None, 1, D_MODEL), (layer, 2, 0, 0)),
                  _param_block((None, D_MODEL, D_IN), (layer, 0, 0)),
                  pl.BlockSpec((3, tm, LANES), lambda bi, i: (0, i, 0))],
        out_specs=[row(A_WIDTH), row(A_WIDTH), row(A_WIDTH),
                   pl.BlockSpec((None, n_ub, tm, LANES), lambda bi, i: (bi, 0, i, 0)),
                   row(C_WIDTH), row(C_KV_WIDTH), row(C_KV_WIDTH)],
        out_shape=[jax.ShapeDtypeStruct((b, t, A_WIDTH), BF16),
                   jax.ShapeDtypeStruct((b, t, A_WIDTH), F32),
                   jax.ShapeDtypeStruct((b, t, A_WIDTH), F32),
                   jax.ShapeDtypeStruct((b, n_ub, t, LANES), F32),
                   jax.ShapeDtypeStruct((b, t, C_WIDTH), BF16),
                   jax.ShapeDtypeStruct((b, t, C_KV_WIDTH), F32),
                   jax.ShapeDtypeStruct((b, t, C_KV_WIDTH), F32)],
        compiler_params=pltpu.CompilerParams(dimension_semantics=("parallel", "parallel"),
                                             vmem_limit_bytes=VMEM_LIMIT_BYTES),
        name="proj",
    )(h, norm_g, win, rope_tab)


def _ssm_prep_kernel(ar_ref, ai_ref, ldt_ref, br_ref, bi_ref, lr_ref, li_ref, bbr_ref, bbi_ref):
    ar, ai = ar_ref[...], ai_ref[...]
    dt = jnp.exp(ldt_ref[...])
    mag = jnp.exp(ar * dt)
    lr, li = mag * jnp.cos(ai * dt), mag * jnp.sin(ai * dt)
    den = ar * ar + ai * ai
    zr = ((lr - 1.0) * ar + li * ai) / den
    zi = (li * ar - (lr - 1.0) * ai) / den
    br, bi = br_ref[...], bi_ref[...]
    lr_ref[...] = lr
    li_ref[...] = li
    bbr_ref[...] = zr * br - zi * bi
    bbi_ref[...] = zr * bi + zi * br


def _ssm_prep(a_re, a_im, log_dt, b_re, b_im):
    depth = a_re.shape[0]
    rows = depth * B_GROUPS * B_GROUP_CH
    rep = lambda a: jnp.repeat(a, B_GROUP_CH, axis=1).reshape(rows, a.shape[-1])
    bt = lambda b: jnp.transpose(b, (0, 1, 3, 2)).reshape(rows, B_STATE)
    sds = jax.ShapeDtypeStruct((rows, B_STATE), F32)
    lr, li, bbr, bbi = pl.pallas_call(
        _ssm_prep_kernel, out_shape=[sds, sds, sds, sds], name="ssm_prep",
    )(rep(a_re), rep(a_im), rep(log_dt[:, :, None]), bt(b_re), bt(b_im))
    per_pair = lambda m: m.reshape(depth, B_GROUPS, B_GROUP_CH, B_STATE)
    return per_pair(lr)[:, :, 0], per_pair(li)[:, :, 0], per_pair(bbr), per_pair(bbi)


def _block_diag(m):
    depth, _, r, c = m.shape
    eye = jnp.eye(B_GROUPS, dtype=m.dtype)
    full = m[:, :, :, None, :] * eye[None, :, None, :, None]
    return full.reshape(depth, B_GROUPS * r, B_GROUPS * c)


def _mixer_kernel(sink_ref,
                  h_ref, qa_ref, kap_ref, kac_ref, vap_ref, vac_ref, bias_ref,
                  qc_ref, kcp_ref, kcc_ref, vcp_ref, vcc_ref,
                  ub_ref, st0_ref, lam_ref, wb_ref, wc_ref, dsk_ref, wglu_ref,
                  gb_ref, wout_ref, g3_ref,
                  o_ref, st_ref,
                  kbuf, vbuf, kcbuf, vcbuf, sa_scr, sc_scr, pa_scr, pc_scr, att_scr,
                  mix_scr, u_scr, v_scr, x_scr, nb_scr, carry_scr,
                  *, rows, chunk, masked, layer):
    i = pl.program_id(1)
    n_chunks = rows // chunk
    band_a = A_BAND + chunk
    band_c = C_BAND + chunk

    kbuf[0:A_BAND, :] = kap_ref[...].astype(BF16)
    kbuf[A_BAND:A_BAND + rows, :] = kac_ref[...].astype(BF16)
    for s in range(N_SLABS):
        vbuf[0:A_BAND, 2 * s * LANES:(2 * s + 1) * LANES] = vap_ref[:, s * LANES:(s + 1) * LANES].astype(BF16)
        vbuf[A_BAND:A_BAND + rows, 2 * s * LANES:(2 * s + 1) * LANES] = (
            vac_ref[:, s * LANES:(s + 1) * LANES].astype(BF16))
        vbuf[:, (2 * s + 1) * LANES:(2 * s + 2) * LANES] = jnp.ones((A_BAND + rows, LANES), BF16)
    kcbuf[0:C_BAND, :] = kcp_ref[...].astype(BF16)
    kcbuf[C_BAND:C_BAND + rows, :] = kcc_ref[...].astype(BF16)
    vcbuf[0:C_BAND, :] = vcp_ref[...].astype(BF16)
    vcbuf[C_BAND:C_BAND + rows, :] = vcc_ref[...].astype(BF16)

    lane = lax.broadcasted_iota(jnp.int32, (1, LANES), 1)
    first_head = lane < HEAD_DIM
    contract_last = (((1,), (1,)), ((), ()))

    contract_first = (((0,), (0,)), ((), ()))
    first_half = lax.broadcasted_iota(jnp.int32, (1, 2 * chunk), 1) < chunk

    def score_chunk(c, slot, mask_early):
        r0 = pl.multiple_of(c * chunk, chunk)
        q_pos = i * rows + c * chunk

        def pair_scores(q_ref, keys, s, back, bias):
            q = q_ref[pl.ds(r0, chunk), s * LANES:(s + 1) * LANES]
            zero = jnp.zeros_like(q)
            both = jnp.concatenate([jnp.where(first_head, q, zero), jnp.where(first_head, zero, q)], axis=0)
            st = lax.dot_general(keys, both, contract_last, preferred_element_type=F32)
            if bias is not None:
                st = st + bias
            if mask_early:
                st = jnp.where(lax.broadcasted_iota(jnp.int32, st.shape, 0) >= back - q_pos, st, NEG_INF)
            return st

        for s in range(N_SLABS):
            sa_scr[slot, s] = pair_scores(qa_ref, kbuf[pl.ds(r0, band_a), s * LANES:(s + 1) * LANES], s, A_BAND,
                                          bias_ref[s])
        keys = kcbuf[pl.ds(r0, band_c), :]
        for s in range(N_SLABS):
            sc_scr[slot, s] = pair_scores(qc_ref, keys, s, C_BAND, None)

    def softmax_chunk(slot):
        for s in range(N_SLABS):
            m = jnp.max(sa_scr[slot, s], axis=0, keepdims=True)
            pa_scr[slot, s] = jnp.exp(sa_scr[slot, s] - m).astype(BF16)
        for s in range(N_SLABS):
            st = sc_scr[slot, s]
            sink = jnp.where(first_half, sink_ref[layer, s], sink_ref[layer, C_GROUP + s])
            m = jnp.maximum(jnp.max(st, axis=0, keepdims=True), sink)
            e = jnp.exp(st - m)
            den = jnp.sum(e, axis=0, keepdims=True) + jnp.exp(sink - m)
            pc_scr[slot, s] = (e * (1.0 / den)).astype(BF16)

    def value_chunk(c, slot):
        r0 = pl.multiple_of(c * chunk, chunk)

        def merge_heads(o):
            return jnp.where(first_head, o[:chunk], o[chunk:])

        for s in range(N_SLABS):
            o2 = lax.dot_general(pa_scr[slot, s], vbuf[pl.ds(r0, band_a), 2 * s * LANES:(2 * s + 2) * LANES],
                                 contract_first, preferred_element_type=F32)
            att_scr[pl.ds(r0, chunk), s * LANES:(s + 1) * LANES] = merge_heads(o2[:, :LANES] * (1.0 / o2[:, LANES:]))
        vals = vcbuf[pl.ds(r0, band_c), :]
        for s in range(N_SLABS):
            o = lax.dot_general(pc_scr[slot, s], vals, contract_first, preferred_element_type=F32)
            att_scr[pl.ds(r0, chunk), A_WIDTH + s * LANES:A_WIDTH + (s + 1) * LANES] = merge_heads(o)

    def run_chunks(mask_early):
        score_chunk(0, 0, mask_early)
        if n_chunks == 1:
            softmax_chunk(0)
            value_chunk(0, 0)
            return

        def pair(g, carry):
            c0 = 2 * g
            score_chunk(c0 + 1, 1, mask_early)
            softmax_chunk(0)
            value_chunk(c0, 0)
            score_chunk(jnp.minimum(c0 + 2, n_chunks - 1), 0, mask_early)
            softmax_chunk(1)
            value_chunk(c0 + 1, 1)
            return carry

        lax.fori_loop(0, n_chunks // 2, pair, 0)

    if masked:
        pl.when(i == 0)(lambda: run_chunks(True))
        pl.when(i != 0)(lambda: run_chunks(False))
    else:
        run_chunks(False)
    mix_scr[:, 0:A_WIDTH] = _rms(att_scr[:, 0:A_WIDTH], gb_ref[:, 0:A_WIDTH]).astype(BF16)
    mix_scr[:, A_WIDTH + B_WIDTH:D_MIX] = _rms(att_scr[:, A_WIDTH:A_WIDTH + C_WIDTH],
                                               gb_ref[:, A_WIDTH + B_WIDTH:D_MIX]).astype(BF16)

    seg = rows // SUBLANES

    @pl.when(i == 0)
    def _():
        carry_scr[...] = st0_ref[...]

    def gather_u(t, carry):
        t8 = pl.multiple_of(t * SUBLANES, SUBLANES)
        for s in range(B_WIDTH // LANES):
            u_scr[pl.ds(t8, SUBLANES), s * LANES:(s + 1) * LANES] = ub_ref[s, pl.ds(t, SUBLANES, stride=seg), :]
        return carry

    lax.fori_loop(0, seg, gather_u, 0, unroll=min(seg, MOVE_UNROLL))
    v_scr[...] = jnp.dot(u_scr[...].astype(BF16), wb_ref[...], preferred_element_type=F32)

    lam_r, lam_i = lam_ref[0:1, :], lam_ref[1:2, :]
    lam_r8 = jnp.broadcast_to(lam_r, (SUBLANES, N_STATE))
    lam_i8 = jnp.broadcast_to(lam_i, (SUBLANES, N_STATE))

    def step(t, xr, xi):
        t8 = pl.multiple_of(t * SUBLANES, SUBLANES)
        vr = v_scr[pl.ds(t8, SUBLANES), 0:N_STATE]
        vi = v_scr[pl.ds(t8, SUBLANES), N_STATE:2 * N_STATE]
        return t8, lam_r8 * xr - lam_i8 * xi + vr, lam_r8 * xi + lam_i8 * xr + vi

    def local_pass(t, carry):
        _, nr, ni = step(t, *carry)
        return nr, ni

    zeros = jnp.zeros((SUBLANES, N_STATE), F32)
    end_r, end_i = lax.fori_loop(0, seg, local_pass, (zeros, zeros), unroll=min(seg, SCAN_UNROLL))

    pw_r, pw_i = lam_r, lam_i
    for _ in range(int(math.log2(seg))):
        pw_r, pw_i = pw_r * pw_r - pw_i * pw_i, 2.0 * pw_r * pw_i
    cr, ci = carry_scr[0:1, :], carry_scr[1:2, :]
    start_r, start_i = [], []
    for j in range(SUBLANES):
        start_r.append(cr)
        start_i.append(ci)
        cr, ci = (end_r[j:j + 1, :] + pw_r * cr - pw_i * ci,
                  end_i[j:j + 1, :] + pw_r * ci + pw_i * cr)
    carry_scr[0:1, :] = cr
    carry_scr[1:2, :] = ci
    st_ref[0:1, :] = cr
    st_ref[1:2, :] = ci

    def full_pass(t, carry):
        t8, nr, ni = step(t, *carry)
        x_scr[pl.ds(t8, SUBLANES), 0:N_STATE] = nr
        x_scr[pl.ds(t8, SUBLANES), N_STATE:2 * N_STATE] = ni
        return nr, ni

    lax.fori_loop(0, seg, full_pass, (jnp.concatenate(start_r, axis=0), jnp.concatenate(start_i, axis=0)),
                  unroll=min(seg, SCAN_UNROLL))

    y = jnp.dot(x_scr[...].astype(BF16), wc_ref[...], preferred_element_type=F32) + dsk_ref[...] * u_scr[...]
    gl = jax.nn.gelu(y, approximate=True)
    g12 = jnp.dot(gl.astype(BF16), wglu_ref[...], preferred_element_type=F32)
    ob = g12[:, :B_WIDTH] * jax.nn.sigmoid(g12[:, B_WIDTH:])
    u_scr[...] = _rms(ob, gb_ref[:, A_WIDTH:A_WIDTH + B_WIDTH])

    def scatter_nb(t, carry):
        t8 = pl.multiple_of(t * SUBLANES, SUBLANES)
        for s in range(B_WIDTH // LANES):
            nb_scr[s, pl.ds(t, SUBLANES, stride=seg), :] = u_scr[pl.ds(t8, SUBLANES), s * LANES:(s + 1) * LANES]
        return carry

    lax.fori_loop(0, seg, scatter_nb, 0)
    for s in range(B_WIDTH // LANES):
        mix_scr[:, A_WIDTH + s * LANES:A_WIDTH + (s + 1) * LANES] = nb_scr[s].astype(BF16)

    out = jnp.dot(mix_scr[...], wout_ref[...], preferred_element_type=F32)
    o_ref[...] = h_ref[...] + _rms(out, g3_ref[...])


def _mixer(h, qa, ka, va, ka_prev, va_prev, bias, qc, kc, vc, kc_prev, vc_prev, ub, st0, p, *,
           layer, prev_layer, rows, chunk, from_cache):
    b, t, _ = h.shape
    n_ub = B_WIDTH // LANES
    assert t % rows == 0 and rows % chunk == 0 and rows % SUBLANES == 0
    assert (rows // SUBLANES) & (rows // SUBLANES - 1) == 0
    row = lambda w: pl.BlockSpec((None, rows, w), lambda bi, i: (bi, i, 0))
    if from_cache:
        assert t == rows
        prev = lambda n, w: pl.BlockSpec((None, None, n, w), lambda bi, i: (prev_layer, bi, 0, 0))
    else:
        assert rows % A_BAND == 0
        prev = lambda n, w: pl.BlockSpec(
            (None, None, n, w), lambda bi, i: (prev_layer, bi, jnp.maximum(i * (rows // n) - 1, 0), 0))
    state_in = pl.BlockSpec((None, None, 2, N_STATE), lambda bi, i: (prev_layer, bi, 0, 0))
    state = pl.BlockSpec((None, 2, N_STATE), lambda bi, i: (bi, 0, 0))
    n_chunks = rows // chunk
    assert n_chunks == 1 or n_chunks % 2 == 0
    slots = min(2, n_chunks)
    kernel = functools.partial(_mixer_kernel, rows=rows, chunk=chunk, masked=not from_cache, layer=layer)
    return pl.pallas_call(
        kernel,
        grid=(b, t // rows),
        in_specs=[pl.BlockSpec(memory_space=pltpu.SMEM),
                  row(D_MODEL), row(A_WIDTH), prev(A_BAND, A_WIDTH), row(A_WIDTH), prev(A_BAND, A_WIDTH), row(A_WIDTH),
                  _param_block((None,) + bias.shape[1:], (layer, 0, 0, 0)),
                  row(C_WIDTH), prev(C_BAND, C_KV_WIDTH), row(C_KV_WIDTH), prev(C_BAND, C_KV_WIDTH), row(C_KV_WIDTH),
                  pl.BlockSpec((None, n_ub, rows, LANES), lambda bi, i: (bi, 0, i, 0)),
                  state_in,
                  _param_block((None, 2, N_STATE), (layer, 0, 0)),
                  _param_block((None, B_WIDTH, 2 * N_STATE), (layer, 0, 0)),
                  _param_block((None, 2 * N_STATE, B_WIDTH), (layer, 0, 0)),
                  _param_block((None, 1, B_WIDTH), (layer, 0, 0)),
                  _param_block((None, B_WIDTH, 2 * B_WIDTH), (layer, 0, 0)),
                  _param_block((None, 1, D_MIX), (layer, 0, 0)),
                  _param_block((None, D_MIX, D_MODEL), (layer, 0, 0)),
                  _param_block((None, None, 1, D_MODEL), (layer, 3, 0, 0))],
        out_specs=[row(D_MODEL), state],
        out_shape=[jax.ShapeDtypeStruct((b, t, D_MODEL), F32),
                   jax.ShapeDtypeStruct((b, 2, N_STATE), F32)],
        scratch_shapes=[pltpu.VMEM((A_BAND + rows, A_WIDTH), BF16),
                        pltpu.VMEM((A_BAND + rows, 2 * A_WIDTH), BF16),
                        pltpu.VMEM((C_BAND + rows, C_KV_WIDTH), BF16),
                        pltpu.VMEM((C_BAND + rows, C_KV_WIDTH), BF16),
                        pltpu.VMEM((slots, N_SLABS, A_BAND + chunk, 2 * chunk), F32),
                        pltpu.VMEM((slots, N_SLABS, C_BAND + chunk, 2 * chunk), F32),
                        pltpu.VMEM((slots, N_SLABS, A_BAND + chunk, 2 * chunk), BF16),
                        pltpu.VMEM((slots, N_SLABS, C_BAND + chunk, 2 * chunk), BF16),
                        pltpu.VMEM((rows, A_WIDTH + C_WIDTH), F32),
                        pltpu.VMEM((rows, D_MIX), BF16),
                        pltpu.VMEM((rows, B_WIDTH), F32),
                        pltpu.VMEM((rows, 2 * N_STATE), F32),
                        pltpu.VMEM((rows, 2 * N_STATE), F32),
                        pltpu.VMEM((n_ub, rows, LANES), F32),
                        pltpu.VMEM((2, N_STATE), F32)],
        compiler_params=pltpu.CompilerParams(dimension_semantics=("parallel", "arbitrary"),
                                             vmem_limit_bytes=VMEM_LIMIT_BYTES),
        name="mixer",
    )(p['sink'], h, qa, ka_prev, ka, va_prev, va, bias, qc, kc_prev, kc, vc_prev, vc, ub, st0,
      p['lam'], p['wb'], p['wc'], p['dsk'], p['wglu'], p['gb'], p['wout'], p['norm_g'])


def _rope_table(pos):
    half = ROPE_DIM // 2
    inv_freq = ROPE_THETA ** (-jnp.arange(half, dtype=F32) / half)
    ang = pos.astype(F32)[:, None] * inv_freq[None, :]
    cos, sin = jnp.cos(ang), jnp.sin(ang)
    t = pos.shape[0]
    ones = jnp.ones((t, HEAD_DIM - ROPE_DIM), F32)
    zeros = jnp.zeros((t, HEAD_DIM - ROPE_DIM), F32)
    zh = jnp.zeros((t, half), F32)
    head = lambda *parts: jnp.concatenate(parts + parts, axis=-1)
    return jnp.stack([head(cos, cos, ones), head(-sin, zh, zeros), head(zh, sin, zeros)])


def _rel_bias(table, chunk):
    n_heads = table.shape[0]
    period = A_BAND + 2 * CHUNK
    top = table[:, 2 * REL_CLIP:]
    ramp = jnp.flip(table[:, REL_CLIP - CHUNK + 1:2 * REL_CLIP], axis=1)
    n_far = A_BAND - REL_CLIP + 1
    row = jnp.concatenate([jnp.broadcast_to(top, (n_heads, n_far)), ramp,
                           jnp.broadcast_to(top, (n_heads, period - n_far - ramp.shape[1]))], axis=1)
    rolled = jnp.tile(row, (1, CHUNK))[:, :CHUNK * (period - 1)].reshape(n_heads, CHUNK, period - 1)
    return rolled[:, :chunk, :A_BAND + chunk]


def _rel_bias_pairs(tables, chunk):
    depth = tables.shape[0]
    band = A_BAND + chunk
    bias = _rel_bias(tables.reshape(depth * A_HEADS, -1), chunk).reshape(depth, N_SLABS, 2, chunk, band)
    return jnp.transpose(bias, (0, 1, 4, 2, 3)).reshape(depth, N_SLABS, band, 2 * chunk)


def _permute_c_heads(a, axis):
    shape = a.shape
    a = a.reshape(shape[:axis] + (C_KV_HEADS, C_GROUP, HEAD_DIM) + shape[axis + 1:])
    return jnp.swapaxes(a, axis, axis + 1).reshape(shape)


def _prep_params(norm_g, branch_norm_g, w_ffn_up, w_ffn_down, w_in, w_out, a_rel_bias, c_sink,
                 ssm_a_re, ssm_a_im, ssm_log_dt, ssm_b_re, ssm_b_im, ssm_c_re, ssm_c_im, ssm_d, w_glu):
    depth = norm_g.shape[0]
    n_ab = A_WIDTH + B_WIDTH
    win = jnp.concatenate([w_in[:, :, :_O_QC], _permute_c_heads(w_in[:, :, _O_QC:_O_KC], 2), w_in[:, :, _O_KC:]],
                          axis=2)
    gb = jnp.concatenate([branch_norm_g[:, :n_ab], _permute_c_heads(branch_norm_g[:, n_ab:], 1)], axis=1)
    wout = jnp.concatenate([w_out[:, :n_ab], _permute_c_heads(w_out[:, n_ab:], 1)], axis=1)
    lr, li, bbr, bbi = _ssm_prep(ssm_a_re, ssm_a_im, ssm_log_dt, ssm_b_re, ssm_b_im)
    wb = jnp.concatenate([_block_diag(bbr), _block_diag(bbi)], axis=2)
    ct = lambda c: jnp.transpose(c, (0, 1, 3, 2))
    wc = jnp.concatenate([_block_diag(ct(ssm_c_re)), -_block_diag(ct(ssm_c_im))], axis=1)
    return dict(
        norm_g=norm_g[:, :, None, :], wup=w_ffn_up.astype(BF16), wdn=w_ffn_down.astype(BF16),
        win=win.astype(BF16), wout=wout.astype(BF16), gb=gb[:, None, :],
        table=a_rel_bias, sink=c_sink.reshape(depth, C_HEADS),
        lam=jnp.stack([lr.reshape(depth, N_STATE), li.reshape(depth, N_STATE)], axis=1),
        wb=wb.astype(BF16), wc=wc.astype(BF16), dsk=ssm_d.reshape(depth, 1, B_WIDTH),
        wglu=w_glu.astype(BF16))


def _layer(x, rope_tab, p, layer, bias, cache, *, tm_ffn, tm_proj, rows, chunk):
    b, t, _ = x.shape
    ffn = lambda a, which: _ffn(a.reshape(b * t, D_MODEL), p['norm_g'], p['wup'], p['wdn'], layer, which,
                                tm_ffn).reshape(b, t, D_MODEL)
    h = ffn(x, 0)
    qa, ka, va, ub, qc, kc, vc = _proj(h, p['norm_g'], p['win'], rope_tab, layer, tm_proj)
    if cache is None:
        ka_prev, va_prev, kc_prev, vc_prev = ka[None], va[None], kc[None], vc[None]
        st0 = jnp.zeros((1, b, 2, N_STATE), F32)
        prev_layer = 0
    else:
        ka_prev, va_prev, kc_prev, vc_prev, st0 = cache
        prev_layer = layer
    h, st = _mixer(h, qa, ka, va, ka_prev, va_prev, bias, qc, kc, vc, kc_prev, vc_prev, ub, st0, p,
                   layer=layer, prev_layer=prev_layer, rows=rows, chunk=chunk, from_cache=cache is not None)
    h = ffn(h, 1)
    if cache is None:
        nak, nav = ka[:, t - A_BAND:], va[:, t - A_BAND:]
        nck, ncv = kc[:, t - C_BAND:], vc[:, t - C_BAND:]
    else:
        nak = jnp.concatenate([ka_prev[layer, :, t:], ka], axis=1)
        nav = jnp.concatenate([va_prev[layer, :, t:], va], axis=1)
        nck = jnp.concatenate([kc_prev[layer, :, t:], kc], axis=1)
        ncv = jnp.concatenate([vc_prev[layer, :, t:], vc], axis=1)
    states = (nak.reshape(b, A_BAND, A_HEADS, HEAD_DIM), nav.reshape(b, A_BAND, A_HEADS, HEAD_DIM),
              nck.reshape(b, C_BAND, C_KV_HEADS, HEAD_DIM), ncv.reshape(b, C_BAND, C_KV_HEADS, HEAD_DIM),
              st[:, 0].reshape(b, B_GROUPS, B_STATE), st[:, 1].reshape(b, B_GROUPS, B_STATE))
    return h, states


def kernel(x_prompt, x_sample, cache_a_k, cache_a_v, cache_c_k, cache_c_v, state_ssm_re, state_ssm_im, norm_g, branch_norm_g, w_ffn_up, w_ffn_down, w_in, w_out, a_rel_bias, c_sink, ssm_a_re, ssm_a_im, ssm_log_dt, ssm_b_re, ssm_b_im, ssm_c_re, ssm_c_im, ssm_d, w_glu):
    depth = norm_g.shape[0]
    t_p, t_s = x_prompt.shape[1], x_sample.shape[1]
    assert t_p >= A_BAND and t_p % A_BAND == 0 and t_s <= CHUNK
    assert cache_a_k.shape[2] == A_BAND and cache_c_k.shape[2] == C_BAND
    rope_p = _rope_table(jnp.arange(t_p))
    rope_s = _rope_table(PAST_LEN + jnp.arange(t_s))
    b_s = x_sample.shape[0]
    p = _prep_params(norm_g, branch_norm_g, w_ffn_up, w_ffn_down, w_in, w_out, a_rel_bias, c_sink,
                     ssm_a_re, ssm_a_im, ssm_log_dt, ssm_b_re, ssm_b_im, ssm_c_re, ssm_c_im, ssm_d, w_glu)
    bias_p = _rel_bias_pairs(p['table'], CHUNK)
    bias_s = _rel_bias_pairs(p['table'], t_s)
    cache = (cache_a_k.reshape(depth, b_s, A_BAND, A_WIDTH), cache_a_v.reshape(depth, b_s, A_BAND, A_WIDTH),
             cache_c_k.reshape(depth, b_s, C_BAND, C_KV_WIDTH), cache_c_v.reshape(depth, b_s, C_BAND, C_KV_WIDTH),
             jnp.stack([state_ssm_re.reshape(depth, b_s, N_STATE), state_ssm_im.reshape(depth, b_s, N_STATE)],
                       axis=2))
    yp, ys = x_prompt, x_sample
    st_p, st_s = [], []
    for l in range(depth):
        yp, sp = _layer(yp, rope_p, p, l, bias_p, None, tm_ffn=512, tm_proj=512, rows=A_BAND, chunk=CHUNK)
        ys, ss = _layer(ys, rope_s, p, l, bias_s, cache, tm_ffn=b_s * t_s, tm_proj=t_s, rows=t_s, chunk=t_s)
        st_p.append(sp)
        st_s.append(ss)
    stack = lambda sts, k: jnp.stack([s[k] for s in sts])
    return (yp, ys) + tuple(stack(st_p, k) for k in range(6)) + tuple(stack(st_s, k) for k in range(6))
```

```python
import functools
import math

import jax
import jax.numpy as jnp
import numpy as np
from jax import lax
from jax.experimental import pallas as pl
from jax.experimental.pallas import tpu as pltpu

F32 = jnp.float32
BF16 = jnp.bfloat16

D_MODEL = 1024
HEAD_DIM = 64
CHUNK = 64
A_HEADS = 6
A_WIDTH = A_HEADS * HEAD_DIM
A_BAND = 8 * CHUNK
REL_CLIP = 128
B_GROUPS = 16
B_GROUP_CH = 16
B_WIDTH = B_GROUPS * B_GROUP_CH
B_STATE = 64
N_STATE = B_GROUPS * B_STATE
C_KV_HEADS = 2
C_GROUP = 3
C_HEADS = C_KV_HEADS * C_GROUP
C_WIDTH = C_HEADS * HEAD_DIM
N_SLABS = 3
SCAN_UNROLL = 2
MOVE_UNROLL = 4
C_KV_WIDTH = C_KV_HEADS * HEAD_DIM
C_BAND = 128
ROPE_THETA = 500000.0
ROPE_DIM = HEAD_DIM // 4
D_MIX = A_WIDTH + B_WIDTH + C_WIDTH
D_IN = 3 * A_WIDTH + B_WIDTH + C_WIDTH + 2 * C_KV_WIDTH
D_FF = 2816
EPS = 1e-6
NEG_INF = -1e30
ATTN_SCALE = HEAD_DIM ** -0.5
PAST_LEN = 1024

LANES = 128
SUBLANES = 8
VMEM_LIMIT_BYTES = 56 * 1024 * 1024

_O_QA, _O_KA, _O_VA = 0, A_WIDTH, 2 * A_WIDTH
_O_UB = 3 * A_WIDTH
_O_QC = _O_UB + B_WIDTH
_O_KC = _O_QC + C_WIDTH
_O_VC = _O_KC + C_KV_WIDTH


def _rms(x, g):
    return x * lax.rsqrt(jnp.mean(x * x, axis=-1, keepdims=True) + EPS) * g


def _param_block(block_shape, index):
    assert len(block_shape) == len(index)
    return pl.BlockSpec(block_shape, lambda *_: index, pipeline_mode=pl.Buffered(1))


def _ffn_kernel(x_ref, g_pre_ref, g_post_ref, wup_ref, wdn_ref, o_ref):
    x = x_ref[...]
    xn = _rms(x, g_pre_ref[...]).astype(BF16)
    mid = jnp.dot(xn, wup_ref[...], preferred_element_type=F32)
    gate, up = mid[:, :D_FF], mid[:, D_FF:]
    act = (gate * jax.nn.sigmoid(gate) * up).astype(BF16)
    y = jnp.dot(act, wdn_ref[...], preferred_element_type=F32)
    o_ref[...] = x + 0.5 * _rms(y, g_post_ref[...])


def _ffn(x2d, norm_g, wup, wdn, layer, which, tm):
    n = x2d.shape[0]
    gain = lambda k: _param_block((None, None, 1, D_MODEL), (layer, k, 0, 0))
    return pl.pallas_call(
        _ffn_kernel,
        grid=(n // tm,),
        in_specs=[pl.BlockSpec((tm, D_MODEL), lambda i: (i, 0)),
                  gain(4 * which), gain(4 * which + 1),
                  _param_block((None, None, D_MODEL, 2 * D_FF), (layer, which, 0, 0)),
                  _param_block((None, None, D_FF, D_MODEL), (layer, which, 0, 0))],
        out_specs=pl.BlockSpec((tm, D_MODEL), lambda i: (i, 0)),
        out_shape=jax.ShapeDtypeStruct((n, D_MODEL), F32),
        compiler_params=pltpu.CompilerParams(dimension_semantics=("parallel",),
                                             vmem_limit_bytes=VMEM_LIMIT_BYTES),
        name="ffn",
    )(x2d, norm_g, norm_g, wup, wdn)


def _rope_slab(x, cos, sin_hi, sin_lo):
    half = ROPE_DIM // 2
    return (x * cos + pltpu.roll(x, LANES - half, 1) * sin_hi + pltpu.roll(x, half, 1) * sin_lo)


def _proj_kernel(h_ref, g_ref, win_ref, rope_ref,
                 qa_ref, ka_ref, va_ref, ub_ref, qc_ref, kc_ref, vc_ref):
    hn = _rms(h_ref[...], g_ref[...]).astype(BF16)
    p = jnp.dot(hn, win_ref[...], preferred_element_type=F32)
    cos, sin_hi, sin_lo = rope_ref[0], rope_ref[1], rope_ref[2]
    qa_ref[...] = (p[:, _O_QA:_O_QA + A_WIDTH] * ATTN_SCALE).astype(BF16)
    ka_ref[...] = p[:, _O_KA:_O_KA + A_WIDTH]
    va_ref[...] = p[:, _O_VA:_O_VA + A_WIDTH]
    for s in range(B_WIDTH // LANES):
        ub_ref[s] = p[:, _O_UB + s * LANES:_O_UB + (s + 1) * LANES]
    for s in range(C_WIDTH // LANES):
        q = p[:, _O_QC + s * LANES:_O_QC + (s + 1) * LANES]
        qc_ref[:, s * LANES:(s + 1) * LANES] = (_rope_slab(q, cos, sin_hi, sin_lo) * ATTN_SCALE).astype(BF16)
    kc_ref[...] = _rope_slab(p[:, _O_KC:_O_KC + C_KV_WIDTH], cos, sin_hi, sin_lo)
    vc_ref[...] = p[:, _O_VC:_O_VC + C_KV_WIDTH]


def _proj(h, norm_g, win, rope_tab, layer, tm):
    b, t, _ = h.shape
    row = lambda w: pl.BlockSpec((None, tm, w), lambda bi, i: (bi, i, 0))
    n_ub = B_WIDTH // LANES
    return pl.pallas_call(
        _proj_kernel,
        grid=(b, t // tm),
        in_specs=[row(D_MODEL),
                  _param_block((None, None, 1, D_MODEL), (layer, 2, 0, 0)),
                  _param_block((None, D_MODEL, D_IN), (layer, 0, 0)),
                  pl.BlockSpec((3, tm, LANES), lambda bi, i: (0, i, 0))],
        out_specs=[row(A_WIDTH), row(A_WIDTH), row(A_WIDTH),
                   pl.BlockSpec((None, n_ub, tm, LANES), lambda bi, i: (bi, 0, i, 0)),
                   row(C_WIDTH), row(C_KV_WIDTH), row(C_KV_WIDTH)],
        out_shape=[jax.ShapeDtypeStruct((b, t, A_WIDTH), BF16),
                   jax.ShapeDtypeStruct((b, t, A_WIDTH), F32),
                   jax.ShapeDtypeStruct((b, t, A_WIDTH), F32),
                   jax.ShapeDtypeStruct((b, n_ub, t, LANES), F32),
                   jax.ShapeDtypeStruct((b, t, C_WIDTH), BF16),
                   jax.ShapeDtypeStruct((b, t, C_KV_WIDTH), F32),
                   jax.ShapeDtypeStruct((b, t, C_KV_WIDTH), F32)],
        compiler_params=pltpu.CompilerParams(dimension_semantics=("parallel", "parallel"),
                                             vmem_limit_bytes=VMEM_LIMIT_BYTES),
        name="proj",
    )(h, norm_g, win, rope_tab)


def _ssm_prep_kernel(ar_ref, ai_ref, ldt_ref, br_ref, bi_ref, lr_ref, li_ref, bbr_ref, bbi_ref):
    ar, ai = ar_ref[...], ai_ref[...]
    dt = jnp.exp(ldt_ref[...])
    mag = jnp.exp(ar * dt)
    lr, li = mag * jnp.cos(ai * dt), mag * jnp.sin(ai * dt)
    den = ar * ar + ai * ai
    zr = ((lr - 1.0) * ar + li * ai) / den
    zi = (li * ar - (lr - 1.0) * ai) / den
    br, bi = br_ref[...], bi_ref[...]
    lr_ref[...] = lr
    li_ref[...] = li
    bbr_ref[...] = zr * br - zi * bi
    bbi_ref[...] = zr * bi + zi * br


def _ssm_prep(a_re, a_im, log_dt, b_re, b_im):
    depth = a_re.shape[0]
    rows = depth * B_GROUPS * B_GROUP_CH
    rep = lambda a: jnp.repeat(a, B_GROUP_CH, axis=1).reshape(rows, a.shape[-1])
    bt = lambda b: jnp.transpose(b, (0, 1, 3, 2)).reshape(rows, B_STATE)
    sds = jax.ShapeDtypeStruct((rows, B_STATE), F32)
    lr, li, bbr, bbi = pl.pallas_call(
        _ssm_prep_kernel, out_shape=[sds, sds, sds, sds], name="ssm_prep",
    )(rep(a_re), rep(a_im), rep(log_dt[:, :, None]), bt(b_re), bt(b_im))
    per_pair = lambda m: m.reshape(depth, B_GROUPS, B_GROUP_CH, B_STATE)
    return per_pair(lr)[:, :, 0], per_pair(li)[:, :, 0], per_pair(bbr), per_pair(bbi)


def _block_diag(m):
    depth, _, r, c = m.shape
    eye = jnp.eye(B_GROUPS, dtype=m.dtype)
    full = m[:, :, :, None, :] * eye[None, :, None, :, None]
    return full.reshape(depth, B_GROUPS * r, B_GROUPS * c)


def _mixer_kernel(sink_ref,
                  h_ref, qa_ref, kap_ref, kac_ref, vap_ref, vac_ref, bias_ref,
                  qc_ref, kcp_ref, kcc_ref, vcp_ref, vcc_ref,
                  ub_ref, st0_ref, lam_ref, wb_ref, wc_ref, dsk_ref, wglu_ref,
                  gb_ref, wout_ref, g3_ref,
                  o_ref, st_ref,
                  kbuf, vbuf, kcbuf, vcbuf, sa_scr, sc_scr, pa_scr, pc_scr, att_scr,
                  mix_scr, u_scr, v_scr, x_scr, nb_scr, carry_scr,
                  *, rows, chunk, masked, layer):
    i = pl.program_id(1)
    n_chunks = rows // chunk
    band_a = A_BAND + chunk
    band_c = C_BAND + chunk

    kbuf[0:A_BAND, :] = kap_ref[...].astype(BF16)
    kbuf[A_BAND:A_BAND + rows, :] = kac_ref[...].astype(BF16)
    for s in range(N_SLABS):
        vbuf[0:A_BAND, 2 * s * LANES:(2 * s + 1) * LANES] = vap_ref[:, s * LANES:(s + 1) * LANES].astype(BF16)
        vbuf[A_BAND:A_BAND + rows, 2 * s * LANES:(2 * s + 1) * LANES] = (
            vac_ref[:, s * LANES:(s + 1) * LANES].astype(BF16))
        vbuf[:, (2 * s + 1) * LANES:(2 * s + 2) * LANES] = jnp.ones((A_BAND + rows, LANES), BF16)
    kcbuf[0:C_BAND, :] = kcp_ref[...].astype(BF16)
    kcbuf[C_BAND:C_BAND + rows, :] = kcc_ref[...].astype(BF16)
    vcbuf[0:C_BAND, :] = vcp_ref[...].astype(BF16)
    vcbuf[C_BAND:C_BAND + rows, :] = vcc_ref[...].astype(BF16)

    lane = lax.broadcasted_iota(jnp.int32, (1, LANES), 1)
    first_head = lane < HEAD_DIM
    contract_last = (((1,), (1,)), ((), ()))

    contract_first = (((0,), (0,)), ((), ()))
    first_half = lax.broadcasted_iota(jnp.int32, (1, 2 * chunk), 1) < chunk

    def score_chunk(c, slot, mask_early):
        r0 = c * chunk
        q_pos = i * rows + c * chunk

        def pair_scores(q_ref, keys, s, back, bias):
            q = q_ref[pl.ds(r0, chunk), s * LANES:(s + 1) * LANES]
            zero = jnp.zeros_like(q)
            both = jnp.concatenate([jnp.where(first_head, q, zero), jnp.where(first_head, zero, q)], axis=0)
            st = lax.dot_general(keys, both, contract_last, preferred_element_type=F32)
            if bias is not None:
                st = st + bias
            if mask_early:
                st = jnp.where(lax.broadcasted_iota(jnp.int32, st.shape, 0) >= back - q_pos, st, NEG_INF)
            return st

        for s in range(N_SLABS):
            sa_scr[slot, s] = pair_scores(qa_ref, kbuf[pl.ds(r0, band_a), s * LANES:(s + 1) * LANES], s, A_BAND,
                                          bias_ref[s])
        keys = kcbuf[pl.ds(r0, band_c), :]
        for s in range(N_SLABS):
            sc_scr[slot, s] = pair_scores(qc_ref, keys, s, C_BAND, None)

    def softmax_chunk(slot):
        for s in range(N_SLABS):
            m = jnp.max(sa_scr[slot, s], axis=0, keepdims=True)
            pa_scr[slot, s] = jnp.exp(sa_scr[slot, s] - m).astype(BF16)
        for s in range(N_SLABS):
            st = sc_scr[slot, s]
            sink = jnp.where(first_half, sink_ref[layer, s], sink_ref[layer, C_GROUP + s])
            m = jnp.maximum(jnp.max(st, axis=0, keepdims=True), sink)
            e = jnp.exp(st - m)
            den = jnp.sum(e, axis=0, keepdims=True) + jnp.exp(sink - m)
            pc_scr[slot, s] = (e * (1.0 / den)).astype(BF16)

    def value_chunk(c, slot):
        r0 = c * chunk

        def merge_heads(o):
            return jnp.where(first_head, o[:chunk], o[chunk:])

        for s in range(N_SLABS):
            o2 = lax.dot_general(pa_scr[slot, s], vbuf[pl.ds(r0, band_a), 2 * s * LANES:(2 * s + 2) * LANES],
                                 contract_first, preferred_element_type=F32)
            att_scr[pl.ds(r0, chunk), s * LANES:(s + 1) * LANES] = merge_heads(o2[:, :LANES] * (1.0 / o2[:, LANES:]))
        vals = vcbuf[pl.ds(r0, band_c), :]
        for s in range(N_SLABS):
            o = lax.dot_general(pc_scr[slot, s], vals, contract_first, preferred_element_type=F32)
            att_scr[pl.ds(r0, chunk), A_WIDTH + s * LANES:A_WIDTH + (s + 1) * LANES] = merge_heads(o)

    def run_chunks(mask_early):
        score_chunk(0, 0, mask_early)
        for c in range(n_chunks):
            if c + 1 < n_chunks:
                score_chunk(c + 1, (c + 1) % 2, mask_early)
            softmax_chunk(c % 2)
            value_chunk(c, c % 2)

    if masked:
        pl.when(i == 0)(lambda: run_chunks(True))
        pl.when(i != 0)(lambda: run_chunks(False))
    else:
        run_chunks(False)
    mix_scr[:, 0:A_WIDTH] = _rms(att_scr[:, 0:A_WIDTH], gb_ref[:, 0:A_WIDTH]).astype(BF16)
    mix_scr[:, A_WIDTH + B_WIDTH:D_MIX] = _rms(att_scr[:, A_WIDTH:A_WIDTH + C_WIDTH],
                                               gb_ref[:, A_WIDTH + B_WIDTH:D_MIX]).astype(BF16)

    seg = rows // SUBLANES

    @pl.when(i == 0)
    def _():
        carry_scr[...] = st0_ref[...]

    def gather_u(t, carry):
        t8 = pl.multiple_of(t * SUBLANES, SUBLANES)
        for s in range(B_WIDTH // LANES):
            u_scr[pl.ds(t8, SUBLANES), s * LANES:(s + 1) * LANES] = ub_ref[s, pl.ds(t, SUBLANES, stride=seg), :]
        return carry

    lax.fori_loop(0, seg, gather_u, 0, unroll=min(seg, MOVE_UNROLL))
    v_scr[...] = jnp.dot(u_scr[...].astype(BF16), wb_ref[...], preferred_element_type=F32)

    lam_r, lam_i = lam_ref[0:1, :], lam_ref[1:2, :]
    lam_r8 = jnp.broadcast_to(lam_r, (SUBLANES, N_STATE))
    lam_i8 = jnp.broadcast_to(lam_i, (SUBLANES, N_STATE))

    def step(t, xr, xi):
        t8 = pl.multiple_of(t * SUBLANES, SUBLANES)
        vr = v_scr[pl.ds(t8, SUBLANES), 0:N_STATE]
        vi = v_scr[pl.ds(t8, SUBLANES), N_STATE:2 * N_STATE]
        return t8, lam_r8 * xr - lam_i8 * xi + vr, lam_r8 * xi + lam_i8 * xr + vi

    def local_pass(t, carry):
        _, nr, ni = step(t, *carry)
        return nr, ni

    zeros = jnp.zeros((SUBLANES, N_STATE), F32)
    end_r, end_i = lax.fori_loop(0, seg, local_pass, (zeros, zeros), unroll=min(seg, SCAN_UNROLL))

    pw_r, pw_i = lam_r, lam_i
    for _ in range(int(math.log2(seg))):
        pw_r, pw_i = pw_r * pw_r - pw_i * pw_i, 2.0 * pw_r * pw_i
    cr, ci = carry_scr[0:1, :], carry_scr[1:2, :]
    start_r, start_i = [], []
    for j in range(SUBLANES):
        start_r.append(cr)
        start_i.append(ci)
        cr, ci = (end_r[j:j + 1, :] + pw_r * cr - pw_i * ci,
                  end_i[j:j + 1, :] + pw_r * ci + pw_i * cr)
    carry_scr[0:1, :] = cr
    carry_scr[1:2, :] = ci
    st_ref[0:1, :] = cr
    st_ref[1:2, :] = ci

    def full_pass(t, carry):
        t8, nr, ni = step(t, *carry)
        x_scr[pl.ds(t8, SUBLANES), 0:N_STATE] = nr
        x_scr[pl.ds(t8, SUBLANES), N_STATE:2 * N_STATE] = ni
        return nr, ni

    lax.fori_loop(0, seg, full_pass, (jnp.concatenate(start_r, axis=0), jnp.concatenate(start_i, axis=0)),
                  unroll=min(seg, SCAN_UNROLL))

    y = jnp.dot(x_scr[...].astype(BF16), wc_ref[...], preferred_element_type=F32) + dsk_ref[...] * u_scr[...]
    gl = jax.nn.gelu(y, approximate=True)
    g12 = jnp.dot(gl.astype(BF16), wglu_ref[...], preferred_element_type=F32)
    ob = g12[:, :B_WIDTH] * jax.nn.sigmoid(g12[:, B_WIDTH:])
    u_scr[...] = _rms(ob, gb_ref[:, A_WIDTH:A_WIDTH + B_WIDTH])

    def scatter_nb(t, carry):
        t8 = pl.multiple_of(t * SUBLANES, SUBLANES)
        for s in range(B_WIDTH // LANES):
            nb_scr[s, pl.ds(t, SUBLANES, stride=seg), :] = u_scr[pl.ds(t8, SUBLANES), s * LANES:(s + 1) * LANES]
        return carry

    lax.fori_loop(0, seg, scatter_nb, 0)
    for s in range(B_WIDTH // LANES):
        mix_scr[:, A_WIDTH + s * LANES:A_WIDTH + (s + 1) * LANES] = nb_scr[s].astype(BF16)

    out = jnp.dot(mix_scr[...], wout_ref[...], preferred_element_type=F32)
    o_ref[...] = h_ref[...] + _rms(out, g3_ref[...])


def _mixer(h, qa, ka, va, ka_prev, va_prev, bias, qc, kc, vc, kc_prev, vc_prev, ub, st0, p, *,
           layer, prev_layer, rows, chunk, from_cache):
    b, t, _ = h.shape
    n_ub = B_WIDTH // LANES
    assert t % rows == 0 and rows % chunk == 0 and rows % SUBLANES == 0
    assert (rows // SUBLANES) & (rows // SUBLANES - 1) == 0
    row = lambda w: pl.BlockSpec((None, rows, w), lambda bi, i: (bi, i, 0))
    if from_cache:
        assert t == rows
        prev = lambda n, w: pl.BlockSpec((None, None, n, w), lambda bi, i: (prev_layer, bi, 0, 0))
    else:
        assert rows % A_BAND == 0
        prev = lambda n, w: pl.BlockSpec(
            (None, None, n, w), lambda bi, i: (prev_layer, bi, jnp.maximum(i * (rows // n) - 1, 0), 0))
    state_in = pl.BlockSpec((None, None, 2, N_STATE), lambda bi, i: (prev_layer, bi, 0, 0))
    state = pl.BlockSpec((None, 2, N_STATE), lambda bi, i: (bi, 0, 0))
    n_chunks = rows // chunk
    slots = min(2, n_chunks)
    kernel = functools.partial(_mixer_kernel, rows=rows, chunk=chunk, masked=not from_cache, layer=layer)
    return pl.pallas_call(
        kernel,
        grid=(b, t // rows),
        in_specs=[pl.BlockSpec(memory_space=pltpu.SMEM),
                  row(D_MODEL), row(A_WIDTH), prev(A_BAND, A_WIDTH), row(A_WIDTH), prev(A_BAND, A_WIDTH), row(A_WIDTH),
                  _param_block((None,) + bias.shape[1:], (layer, 0, 0, 0)),
                  row(C_WIDTH), prev(C_BAND, C_KV_WIDTH), row(C_KV_WIDTH), prev(C_BAND, C_KV_WIDTH), row(C_KV_WIDTH),
                  pl.BlockSpec((None, n_ub, rows, LANES), lambda bi, i: (bi, 0, i, 0)),
                  state_in,
                  _param_block((None, 2, N_STATE), (layer, 0, 0)),
                  _param_block((None, B_WIDTH, 2 * N_STATE), (layer, 0, 0)),
                  _param_block((None, 2 * N_STATE, B_WIDTH), (layer, 0, 0)),
                  _param_block((None, 1, B_WIDTH), (layer, 0, 0)),
                  _param_block((None, B_WIDTH, 2 * B_WIDTH), (layer, 0, 0)),
                  _param_block((None, 1, D_MIX), (layer, 0, 0)),
                  _param_block((None, D_MIX, D_MODEL), (layer, 0, 0)),
                  _param_block((None, None, 1, D_MODEL), (layer, 3, 0, 0))],
        out_specs=[row(D_MODEL), state],
        out_shape=[jax.ShapeDtypeStruct((b, t, D_MODEL), F32),
                   jax.ShapeDtypeStruct((b, 2, N_STATE), F32)],
        scratch_shapes=[pltpu.VMEM((A_BAND + rows, A_WIDTH), BF16),
                        pltpu.VMEM((A_BAND + rows, 2 * A_WIDTH), BF16),
                        pltpu.VMEM((C_BAND + rows, C_KV_WIDTH), BF16),
                        pltpu.VMEM((C_BAND + rows, C_KV_WIDTH), BF16),
                        pltpu.VMEM((slots, N_SLABS, A_BAND + chunk, 2 * chunk), F32),
                        pltpu.VMEM((slots, N_SLABS, C_BAND + chunk, 2 * chunk), F32),
                        pltpu.VMEM((slots, N_SLABS, A_BAND + chunk, 2 * chunk), BF16),
                        pltpu.VMEM((slots, N_SLABS, C_BAND + chunk, 2 * chunk), BF16),
                        pltpu.VMEM((rows, A_WIDTH + C_WIDTH), F32),
                        pltpu.VMEM((rows, D_MIX), BF16),
                        pltpu.VMEM((rows, B_WIDTH), F32),
                        pltpu.VMEM((rows, 2 * N_STATE), F32),
                        pltpu.VMEM((rows, 2 * N_STATE), F32),
                        pltpu.VMEM((n_ub, rows, LANES), F32),
                        pltpu.VMEM((2, N_STATE), F32)],
        compiler_params=pltpu.CompilerParams(dimension_semantics=("parallel", "arbitrary"),
                                             vmem_limit_bytes=VMEM_LIMIT_BYTES),
        name="mixer",
    )(p['sink'], h, qa, ka_prev, ka, va_prev, va, bias, qc, kc_prev, kc, vc_prev, vc, ub, st0,
      p['lam'], p['wb'], p['wc'], p['dsk'], p['wglu'], p['gb'], p['wout'], p['norm_g'])


def _rope_table(pos):
    half = ROPE_DIM // 2
    inv_freq = ROPE_THETA ** (-np.arange(half, dtype=np.float64) / half)
    ang = np.asarray(pos, np.float64)[:, None] * inv_freq[None, :]
    cos, sin = jnp.asarray(np.cos(ang), F32), jnp.asarray(np.sin(ang), F32)
    t = len(pos)
    ones = jnp.ones((t, HEAD_DIM - ROPE_DIM), F32)
    zeros = jnp.zeros((t, HEAD_DIM - ROPE_DIM), F32)
    zh = jnp.zeros((t, half), F32)
    head = lambda *parts: jnp.concatenate(parts + parts, axis=-1)
    return jnp.stack([head(cos, cos, ones), head(-sin, zh, zeros), head(zh, sin, zeros)])


def _rel_bias(table, chunk):
    n_heads = table.shape[0]
    period = A_BAND + 2 * CHUNK
    top = table[:, 2 * REL_CLIP:]
    ramp = jnp.flip(table[:, REL_CLIP - CHUNK + 1:2 * REL_CLIP], axis=1)
    n_far = A_BAND - REL_CLIP + 1
    row = jnp.concatenate([jnp.broadcast_to(top, (n_heads, n_far)), ramp,
                           jnp.broadcast_to(top, (n_heads, period - n_far - ramp.shape[1]))], axis=1)
    rolled = jnp.tile(row, (1, CHUNK))[:, :CHUNK * (period - 1)].reshape(n_heads, CHUNK, period - 1)
    return rolled[:, :chunk, :A_BAND + chunk]


def _rel_bias_pairs(tables, chunk):
    depth = tables.shape[0]
    band = A_BAND + chunk
    bias = _rel_bias(tables.reshape(depth * A_HEADS, -1), chunk).reshape(depth, N_SLABS, 2, chunk, band)
    return jnp.transpose(bias, (0, 1, 4, 2, 3)).reshape(depth, N_SLABS, band, 2 * chunk)


def _permute_c_heads(a, axis):
    shape = a.shape
    a = a.reshape(shape[:axis] + (C_KV_HEADS, C_GROUP, HEAD_DIM) + shape[axis + 1:])
    return jnp.swapaxes(a, axis, axis + 1).reshape(shape)


def _prep_params(norm_g, branch_norm_g, w_ffn_up, w_ffn_down, w_in, w_out, a_rel_bias, c_sink,
                 ssm_a_re, ssm_a_im, ssm_log_dt, ssm_b_re, ssm_b_im, ssm_c_re, ssm_c_im, ssm_d, w_glu):
    depth = norm_g.shape[0]
    n_ab = A_WIDTH + B_WIDTH
    win = jnp.concatenate([w_in[:, :, :_O_QC], _permute_c_heads(w_in[:, :, _O_QC:_O_KC], 2), w_in[:, :, _O_KC:]],
                          axis=2)
    gb = jnp.concatenate([branch_norm_g[:, :n_ab], _permute_c_heads(branch_norm_g[:, n_ab:], 1)], axis=1)
    wout = jnp.concatenate([w_out[:, :n_ab], _permute_c_heads(w_out[:, n_ab:], 1)], axis=1)
    lr, li, bbr, bbi = _ssm_prep(ssm_a_re, ssm_a_im, ssm_log_dt, ssm_b_re, ssm_b_im)
    wb = jnp.concatenate([_block_diag(bbr), _block_diag(bbi)], axis=2)
    ct = lambda c: jnp.transpose(c, (0, 1, 3, 2))
    wc = jnp.concatenate([_block_diag(ct(ssm_c_re)), -_block_diag(ct(ssm_c_im))], axis=1)
    return dict(
        norm_g=norm_g[:, :, None, :], wup=w_ffn_up.astype(BF16), wdn=w_ffn_down.astype(BF16),
        win=win.astype(BF16), wout=wout.astype(BF16), gb=gb[:, None, :],
        table=a_rel_bias, sink=c_sink.reshape(depth, C_HEADS),
        lam=jnp.stack([lr.reshape(depth, N_STATE), li.reshape(depth, N_STATE)], axis=1),
        wb=wb.astype(BF16), wc=wc.astype(BF16), dsk=ssm_d.reshape(depth, 1, B_WIDTH),
        wglu=w_glu.astype(BF16))


def _layer(x, rope_tab, p, layer, bias, cache, *, tm_ffn, tm_proj, rows, chunk):
    b, t, _ = x.shape
    ffn = lambda a, which: _ffn(a.reshape(b * t, D_MODEL), p['norm_g'], p['wup'], p['wdn'], layer, which,
                                tm_ffn).reshape(b, t, D_MODEL)
    h = ffn(x, 0)
    qa, ka, va, ub, qc, kc, vc = _proj(h, p['norm_g'], p['win'], rope_tab, layer, tm_proj)
    if cache is None:
        ka_prev, va_prev, kc_prev, vc_prev = ka[None], va[None], kc[None], vc[None]
        st0 = jnp.zeros((1, b, 2, N_STATE), F32)
        prev_layer = 0
    else:
        ka_prev, va_prev, kc_prev, vc_prev, st0 = cache
        prev_layer = layer
    h, st = _mixer(h, qa, ka, va, ka_prev, va_prev, bias, qc, kc, vc, kc_prev, vc_prev, ub, st0, p,
                   layer=layer, prev_layer=prev_layer, rows=rows, chunk=chunk, from_cache=cache is not None)
    h = ffn(h, 1)
    if cache is None:
        nak, nav = ka[:, t - A_BAND:], va[:, t - A_BAND:]
        nck, ncv = kc[:, t - C_BAND:], vc[:, t - C_BAND:]
    else:
        nak = jnp.concatenate([ka_prev[layer, :, t:], ka], axis=1)
        nav = jnp.concatenate([va_prev[layer, :, t:], va], axis=1)
        nck = jnp.concatenate([kc_prev[layer, :, t:], kc], axis=1)
        ncv = jnp.concatenate([vc_prev[layer, :, t:], vc], axis=1)
    states = (nak.reshape(b, A_BAND, A_HEADS, HEAD_DIM), nav.reshape(b, A_BAND, A_HEADS, HEAD_DIM),
              nck.reshape(b, C_BAND, C_KV_HEADS, HEAD_DIM), ncv.reshape(b, C_BAND, C_KV_HEADS, HEAD_DIM),
              st[:, 0].reshape(b, B_GROUPS, B_STATE), st[:, 1].reshape(b, B_GROUPS, B_STATE))
    return h, states


def kernel(x_prompt, x_sample, cache_a_k, cache_a_v, cache_c_k, cache_c_v, state_ssm_re, state_ssm_im, norm_g, branch_norm_g, w_ffn_up, w_ffn_down, w_in, w_out, a_rel_bias, c_sink, ssm_a_re, ssm_a_im, ssm_log_dt, ssm_b_re, ssm_b_im, ssm_c_re, ssm_c_im, ssm_d, w_glu):
    depth = norm_g.shape[0]
    t_p, t_s = x_prompt.shape[1], x_sample.shape[1]
    assert t_p >= A_BAND and t_p % A_BAND == 0 and t_s <= CHUNK
    assert cache_a_k.shape[2] == A_BAND and cache_c_k.shape[2] == C_BAND
    rope_p = _rope_table(np.arange(t_p))
    rope_s = _rope_table(PAST_LEN + np.arange(t_s))
    b_s = x_sample.shape[0]
    p = _prep_params(norm_g, branch_norm_g, w_ffn_up, w_ffn_down, w_in, w_out, a_rel_bias, c_sink,
                     ssm_a_re, ssm_a_im, ssm_log_dt, ssm_b_re, ssm_b_im, ssm_c_re, ssm_c_im, ssm_d, w_glu)
    bias_p = _rel_bias_pairs(p['table'], CHUNK)
    bias_s = _rel_bias_pairs(p['table'], t_s)
    cache = (cache_a_k.reshape(depth, b_s, A_BAND, A_WIDTH), cache_a_v.reshape(depth, b_s, A_BAND, A_WIDTH),
             cache_c_k.reshape(depth, b_s, C_BAND, C_KV_WIDTH), cache_c_v.reshape(depth, b_s, C_BAND, C_KV_WIDTH),
             jnp.stack([state_ssm_re.reshape(depth, b_s, N_STATE), state_ssm_im.reshape(depth, b_s, N_STATE)],
                       axis=2))
    yp, ys = x_prompt, x_sample
    st_p, st_s = [], []
    for l in range(depth):
        yp, sp = _layer(yp, rope_p, p, l, bias_p, None, tm_ffn=512, tm_proj=512, rows=A_BAND, chunk=CHUNK)
        ys, ss = _layer(ys, rope_s, p, l, bias_s, cache, tm_ffn=b_s * t_s, tm_proj=t_s, rows=t_s, chunk=t_s)
        st_p.append(sp)
        st_s.append(ss)
    stack = lambda sts, k: jnp.stack([s[k] for s in sts])
    return (yp, ys) + tuple(stack(st_p, k) for k in range(6)) + tuple(stack(st_s, k) for k in range(6))
```

```python
import functools
import math

import jax
import jax.numpy as jnp
import numpy as np
from jax import lax
from jax.experimental import pallas as pl
from jax.experimental.pallas import tpu as pltpu

F32 = jnp.float32
BF16 = jnp.bfloat16

D_MODEL = 1024
HEAD_DIM = 64
CHUNK = 64
A_HEADS = 6
A_WIDTH = A_HEADS * HEAD_DIM
A_BAND = 8 * CHUNK
REL_CLIP = 128
B_GROUPS = 16
B_GROUP_CH = 16
B_WIDTH = B_GROUPS * B_GROUP_CH
B_STATE = 64
N_STATE = B_GROUPS * B_STATE
C_KV_HEADS = 2
C_GROUP = 3
C_HEADS = C_KV_HEADS * C_GROUP
C_WIDTH = C_HEADS * HEAD_DIM
N_SLABS = 3
MOVE_UNROLL = 4
C_KV_WIDTH = C_KV_HEADS * HEAD_DIM
C_BAND = 128
ROPE_THETA = 500000.0
ROPE_DIM = HEAD_DIM // 4
D_MIX = A_WIDTH + B_WIDTH + C_WIDTH
D_IN = 3 * A_WIDTH + B_WIDTH + C_WIDTH + 2 * C_KV_WIDTH
D_FF = 2816
EPS = 1e-6
NEG_INF = -1e30
ATTN_SCALE = HEAD_DIM ** -0.5
PAST_LEN = 1024

LANES = 128
SUBLANES = 8
VMEM_LIMIT_BYTES = 56 * 1024 * 1024

_O_QA, _O_KA, _O_VA = 0, A_WIDTH, 2 * A_WIDTH
_O_UB = 3 * A_WIDTH
_O_QC = _O_UB + B_WIDTH
_O_KC = _O_QC + C_WIDTH
_O_VC = _O_KC + C_KV_WIDTH


def _rms(x, g):
    return x * lax.rsqrt(jnp.mean(x * x, axis=-1, keepdims=True) + EPS) * g


def _param_block(block_shape, index):
    assert len(block_shape) == len(index)
    return pl.BlockSpec(block_shape, lambda *_: index, pipeline_mode=pl.Buffered(1))


def _ffn_kernel(x_ref, g_pre_ref, g_post_ref, wup_ref, wdn_ref, o_ref):
    x = x_ref[...]
    xn = _rms(x, g_pre_ref[...]).astype(BF16)
    mid = jnp.dot(xn, wup_ref[...], preferred_element_type=F32)
    gate, up = mid[:, :D_FF], mid[:, D_FF:]
    act = (gate * jax.nn.sigmoid(gate) * up).astype(BF16)
    y = jnp.dot(act, wdn_ref[...], preferred_element_type=F32)
    o_ref[...] = x + 0.5 * _rms(y, g_post_ref[...])


def _ffn(x2d, norm_g, wup, wdn, layer, which, tm):
    n = x2d.shape[0]
    gain = lambda k: _param_block((None, None, 1, D_MODEL), (layer, k, 0, 0))
    return pl.pallas_call(
        _ffn_kernel,
        grid=(n // tm,),
        in_specs=[pl.BlockSpec((tm, D_MODEL), lambda i: (i, 0)),
                  gain(4 * which), gain(4 * which + 1),
                  _param_block((None, None, D_MODEL, 2 * D_FF), (layer, which, 0, 0)),
                  _param_block((None, None, D_FF, D_MODEL), (layer, which, 0, 0))],
        out_specs=pl.BlockSpec((tm, D_MODEL), lambda i: (i, 0)),
        out_shape=jax.ShapeDtypeStruct((n, D_MODEL), F32),
        compiler_params=pltpu.CompilerParams(dimension_semantics=("parallel",),
                                             vmem_limit_bytes=VMEM_LIMIT_BYTES),
        name="ffn",
    )(x2d, norm_g, norm_g, wup, wdn)


def _rope_slab(x, cos, sin_hi, sin_lo):
    half = ROPE_DIM // 2
    return (x * cos + pltpu.roll(x, LANES - half, 1) * sin_hi + pltpu.roll(x, half, 1) * sin_lo)


def _proj_kernel(h_ref, g_ref, win_ref, rope_ref,
                 qa_ref, ka_ref, va_ref, ub_ref, qc_ref, kc_ref, vc_ref):
    hn = _rms(h_ref[...], g_ref[...]).astype(BF16)
    p = jnp.dot(hn, win_ref[...], preferred_element_type=F32)
    cos, sin_hi, sin_lo = rope_ref[0], rope_ref[1], rope_ref[2]
    qa_ref[...] = (p[:, _O_QA:_O_QA + A_WIDTH] * ATTN_SCALE).astype(BF16)
    ka_ref[...] = p[:, _O_KA:_O_KA + A_WIDTH]
    va_ref[...] = p[:, _O_VA:_O_VA + A_WIDTH]
    for s in range(B_WIDTH // LANES):
        ub_ref[s] = p[:, _O_UB + s * LANES:_O_UB + (s + 1) * LANES]
    for s in range(C_WIDTH // LANES):
        q = p[:, _O_QC + s * LANES:_O_QC + (s + 1) * LANES]
        qc_ref[:, s * LANES:(s + 1) * LANES] = (_rope_slab(q, cos, sin_hi, sin_lo) * ATTN_SCALE).astype(BF16)
    kc_ref[...] = _rope_slab(p[:, _O_KC:_O_KC + C_KV_WIDTH], cos, sin_hi, sin_lo)
    vc_ref[...] = p[:, _O_VC:_O_VC + C_KV_WIDTH]


def _proj(h, norm_g, win, rope_tab, layer, tm):
    b, t, _ = h.shape
    row = lambda w: pl.BlockSpec((None, tm, w), lambda bi, i: (bi, i, 0))
    n_ub = B_WIDTH // LANES
    return pl.pallas_call(
        _proj_kernel,
        grid=(b, t // tm),
        in_specs=[row(D_MODEL),
                  _param_block((None, None, 1, D_MODEL), (layer, 2, 0, 0)),
                  _param_block((None, D_MODEL, D_IN), (layer, 0, 0)),
                  pl.BlockSpec((3, tm, LANES), lambda bi, i: (0, i, 0))],
        out_specs=[row(A_WIDTH), row(A_WIDTH), row(A_WIDTH),
                   pl.BlockSpec((None, n_ub, tm, LANES), lambda bi, i: (bi, 0, i, 0)),
                   row(C_WIDTH), row(C_KV_WIDTH), row(C_KV_WIDTH)],
        out_shape=[jax.ShapeDtypeStruct((b, t, A_WIDTH), BF16),
                   jax.ShapeDtypeStruct((b, t, A_WIDTH), F32),
                   jax.ShapeDtypeStruct((b, t, A_WIDTH), F32),
                   jax.ShapeDtypeStruct((b, n_ub, t, LANES), F32),
                   jax.ShapeDtypeStruct((b, t, C_WIDTH), BF16),
                   jax.ShapeDtypeStruct((b, t, C_KV_WIDTH), F32),
                   jax.ShapeDtypeStruct((b, t, C_KV_WIDTH), F32)],
        compiler_params=pltpu.CompilerParams(dimension_semantics=("parallel", "parallel"),
                                             vmem_limit_bytes=VMEM_LIMIT_BYTES),
        name="proj",
    )(h, norm_g, win, rope_tab)


def _ssm_prep_kernel(ar_ref, ai_ref, ldt_ref, br_ref, bi_ref, lr_ref, li_ref, bbr_ref, bbi_ref):
    ar, ai = ar_ref[...], ai_ref[...]
    dt = jnp.exp(ldt_ref[...])
    mag = jnp.exp(ar * dt)
    lr, li = mag * jnp.cos(ai * dt), mag * jnp.sin(ai * dt)
    den = ar * ar + ai * ai
    zr = ((lr - 1.0) * ar + li * ai) / den
    zi = (li * ar - (lr - 1.0) * ai) / den
    br, bi = br_ref[...], bi_ref[...]
    lr_ref[...] = lr
    li_ref[...] = li
    bbr_ref[...] = zr * br - zi * bi
    bbi_ref[...] = zr * bi + zi * br


def _ssm_prep(a_re, a_im, log_dt, b_re, b_im):
    depth = a_re.shape[0]
    rows = depth * B_GROUPS * B_GROUP_CH
    rep = lambda a: jnp.repeat(a, B_GROUP_CH, axis=1).reshape(rows, a.shape[-1])
    bt = lambda b: jnp.transpose(b, (0, 1, 3, 2)).reshape(rows, B_STATE)
    sds = jax.ShapeDtypeStruct((rows, B_STATE), F32)
    lr, li, bbr, bbi = pl.pallas_call(
        _ssm_prep_kernel, out_shape=[sds, sds, sds, sds], name="ssm_prep",
    )(rep(a_re), rep(a_im), rep(log_dt[:, :, None]), bt(b_re), bt(b_im))
    per_pair = lambda m: m.reshape(depth, B_GROUPS, B_GROUP_CH, B_STATE)
    return per_pair(lr)[:, :, 0], per_pair(li)[:, :, 0], per_pair(bbr), per_pair(bbi)


def _block_diag(m):
    depth, _, r, c = m.shape
    eye = jnp.eye(B_GROUPS, dtype=m.dtype)
    full = m[:, :, :, None, :] * eye[None, :, None, :, None]
    return full.reshape(depth, B_GROUPS * r, B_GROUPS * c)


def _mixer_kernel(sink_ref,
                  h_ref, qa_ref, kap_ref, kac_ref, vap_ref, vac_ref, bias_ref,
                  qc_ref, kcp_ref, kcc_ref, vcp_ref, vcc_ref,
                  ub_ref, st0_ref, lam_ref, wb_ref, wc_ref, dsk_ref, wglu_ref,
                  gb_ref, wout_ref, g3_ref,
                  o_ref, st_ref,
                  kbuf, vbuf, kcbuf, vcbuf, sa_scr, sc_scr, pa_scr, pc_scr, att_scr,
                  mix_scr, u_scr, v_scr, x_scr, nb_scr, carry_scr, scan_scr,
                  *, rows, chunk, masked, layer):
    i = pl.program_id(1)
    n_chunks = rows // chunk
    band_a = A_BAND + chunk
    band_c = C_BAND + chunk

    kbuf[0:A_BAND, :] = kap_ref[...].astype(BF16)
    kbuf[A_BAND:A_BAND + rows, :] = kac_ref[...].astype(BF16)
    for s in range(N_SLABS):
        vbuf[0:A_BAND, 2 * s * LANES:(2 * s + 1) * LANES] = vap_ref[:, s * LANES:(s + 1) * LANES].astype(BF16)
        vbuf[A_BAND:A_BAND + rows, 2 * s * LANES:(2 * s + 1) * LANES] = (
            vac_ref[:, s * LANES:(s + 1) * LANES].astype(BF16))
        vbuf[:, (2 * s + 1) * LANES:(2 * s + 2) * LANES] = jnp.ones((A_BAND + rows, LANES), BF16)
    kcbuf[0:C_BAND, :] = kcp_ref[...].astype(BF16)
    kcbuf[C_BAND:C_BAND + rows, :] = kcc_ref[...].astype(BF16)
    vcbuf[0:C_BAND, :] = vcp_ref[...].astype(BF16)
    vcbuf[C_BAND:C_BAND + rows, :] = vcc_ref[...].astype(BF16)

    lane = lax.broadcasted_iota(jnp.int32, (1, LANES), 1)
    first_head = lane < HEAD_DIM
    contract_last = (((1,), (1,)), ((), ()))

    contract_first = (((0,), (0,)), ((), ()))
    first_half = lax.broadcasted_iota(jnp.int32, (1, 2 * chunk), 1) < chunk

    def score_chunk(c, slot, mask_early):
        r0 = c * chunk
        q_pos = i * rows + c * chunk

        def pair_scores(q_ref, keys, s, back, bias):
            q = q_ref[pl.ds(r0, chunk), s * LANES:(s + 1) * LANES]
            zero = jnp.zeros_like(q)
            both = jnp.concatenate([jnp.where(first_head, q, zero), jnp.where(first_head, zero, q)], axis=0)
            st = lax.dot_general(keys, both, contract_last, preferred_element_type=F32)
            if bias is not None:
                st = st + bias
            if mask_early:
                st = jnp.where(lax.broadcasted_iota(jnp.int32, st.shape, 0) >= back - q_pos, st, NEG_INF)
            return st

        for s in range(N_SLABS):
            sa_scr[slot, s] = pair_scores(qa_ref, kbuf[pl.ds(r0, band_a), s * LANES:(s + 1) * LANES], s, A_BAND,
                                          bias_ref[s])
        keys = kcbuf[pl.ds(r0, band_c), :]
        for s in range(N_SLABS):
            sc_scr[slot, s] = pair_scores(qc_ref, keys, s, C_BAND, None)

    def softmax_chunk(slot):
        for s in range(N_SLABS):
            m = jnp.max(sa_scr[slot, s], axis=0, keepdims=True)
            pa_scr[slot, s] = jnp.exp(sa_scr[slot, s] - m).astype(BF16)
        for s in range(N_SLABS):
            st = sc_scr[slot, s]
            sink = jnp.where(first_half, sink_ref[layer, s], sink_ref[layer, C_GROUP + s])
            m = jnp.maximum(jnp.max(st, axis=0, keepdims=True), sink)
            e = jnp.exp(st - m)
            den = jnp.sum(e, axis=0, keepdims=True) + jnp.exp(sink - m)
            pc_scr[slot, s] = (e * (1.0 / den)).astype(BF16)

    def value_chunk(c, slot):
        r0 = c * chunk

        def merge_heads(o):
            return jnp.where(first_head, o[:chunk], o[chunk:])

        for s in range(N_SLABS):
            o2 = lax.dot_general(pa_scr[slot, s], vbuf[pl.ds(r0, band_a), 2 * s * LANES:(2 * s + 2) * LANES],
                                 contract_first, preferred_element_type=F32)
            att_scr[pl.ds(r0, chunk), s * LANES:(s + 1) * LANES] = merge_heads(o2[:, :LANES] * (1.0 / o2[:, LANES:]))
        vals = vcbuf[pl.ds(r0, band_c), :]
        for s in range(N_SLABS):
            o = lax.dot_general(pc_scr[slot, s], vals, contract_first, preferred_element_type=F32)
            att_scr[pl.ds(r0, chunk), A_WIDTH + s * LANES:A_WIDTH + (s + 1) * LANES] = merge_heads(o)

    seg = rows // SUBLANES
    steps_per_chunk = (2 * seg) // n_chunks

    @pl.when(i == 0)
    def _():
        carry_scr[...] = st0_ref[...]

    def gather_u(t, carry):
        t8 = pl.multiple_of(t * SUBLANES, SUBLANES)
        for s in range(B_WIDTH // LANES):
            u_scr[pl.ds(t8, SUBLANES), s * LANES:(s + 1) * LANES] = ub_ref[s, pl.ds(t, SUBLANES, stride=seg), :]
        return carry

    lax.fori_loop(0, seg, gather_u, 0, unroll=min(seg, MOVE_UNROLL))
    v_scr[...] = jnp.dot(u_scr[...].astype(BF16), wb_ref[...], preferred_element_type=F32)

    def segment_starts(end_r, end_i):
        lam_r, lam_i = lam_ref[0:1, :], lam_ref[1:2, :]
        pw_r, pw_i = lam_r, lam_i
        for _ in range(int(math.log2(seg))):
            pw_r, pw_i = pw_r * pw_r - pw_i * pw_i, 2.0 * pw_r * pw_i
        cr, ci = carry_scr[0:1, :], carry_scr[1:2, :]
        start_r, start_i = [], []
        for j in range(SUBLANES):
            start_r.append(cr)
            start_i.append(ci)
            cr, ci = (end_r[j:j + 1, :] + pw_r * cr - pw_i * ci,
                      end_i[j:j + 1, :] + pw_r * ci + pw_i * cr)
        carry_scr[0:1, :] = cr
        carry_scr[1:2, :] = ci
        st_ref[0:1, :] = cr
        st_ref[1:2, :] = ci
        return jnp.concatenate(start_r, axis=0), jnp.concatenate(start_i, axis=0)

    def scan_share(k):
        lam_r8 = jnp.broadcast_to(lam_ref[0:1, :], (SUBLANES, N_STATE))
        lam_i8 = jnp.broadcast_to(lam_ref[1:2, :], (SUBLANES, N_STATE))
        if k == 0:
            xr = xi = jnp.zeros((SUBLANES, N_STATE), F32)
        else:
            xr, xi = scan_scr[0], scan_scr[1]
        for idx in range(k * steps_per_chunk, (k + 1) * steps_per_chunk):
            if idx == seg:
                xr, xi = segment_starts(xr, xi)
            t8 = (idx % seg) * SUBLANES
            vr = v_scr[t8:t8 + SUBLANES, 0:N_STATE]
            vi = v_scr[t8:t8 + SUBLANES, N_STATE:2 * N_STATE]
            xr, xi = lam_r8 * xr - lam_i8 * xi + vr, lam_r8 * xi + lam_i8 * xr + vi
            if idx >= seg:
                x_scr[t8:t8 + SUBLANES, 0:N_STATE] = xr
                x_scr[t8:t8 + SUBLANES, N_STATE:2 * N_STATE] = xi
        if k + 1 < n_chunks:
            scan_scr[0] = xr
            scan_scr[1] = xi

    def run_chunks(mask_early):
        score_chunk(0, 0, mask_early)
        for c in range(n_chunks):
            if c + 1 < n_chunks:
                score_chunk(c + 1, (c + 1) % 2, mask_early)
            softmax_chunk(c % 2)
            value_chunk(c, c % 2)
            scan_share(c)

    if masked:
        pl.when(i == 0)(lambda: run_chunks(True))
        pl.when(i != 0)(lambda: run_chunks(False))
    else:
        run_chunks(False)
    mix_scr[:, 0:A_WIDTH] = _rms(att_scr[:, 0:A_WIDTH], gb_ref[:, 0:A_WIDTH]).astype(BF16)
    mix_scr[:, A_WIDTH + B_WIDTH:D_MIX] = _rms(att_scr[:, A_WIDTH:A_WIDTH + C_WIDTH],
                                               gb_ref[:, A_WIDTH + B_WIDTH:D_MIX]).astype(BF16)

    y = jnp.dot(x_scr[...].astype(BF16), wc_ref[...], preferred_element_type=F32) + dsk_ref[...] * u_scr[...]
    gl = jax.nn.gelu(y, approximate=True)
    g12 = jnp.dot(gl.astype(BF16), wglu_ref[...], preferred_element_type=F32)
    ob = g12[:, :B_WIDTH] * jax.nn.sigmoid(g12[:, B_WIDTH:])
    u_scr[...] = _rms(ob, gb_ref[:, A_WIDTH:A_WIDTH + B_WIDTH])

    def scatter_nb(t, carry):
        t8 = pl.multiple_of(t * SUBLANES, SUBLANES)
        for s in range(B_WIDTH // LANES):
            nb_scr[s, pl.ds(t, SUBLANES, stride=seg), :] = u_scr[pl.ds(t8, SUBLANES), s * LANES:(s + 1) * LANES]
        return carry

    lax.fori_loop(0, seg, scatter_nb, 0)
    for s in range(B_WIDTH // LANES):
        mix_scr[:, A_WIDTH + s * LANES:A_WIDTH + (s + 1) * LANES] = nb_scr[s].astype(BF16)

    out = jnp.dot(mix_scr[...], wout_ref[...], preferred_element_type=F32)
    o_ref[...] = h_ref[...] + _rms(out, g3_ref[...])


def _mixer(h, qa, ka, va, ka_prev, va_prev, bias, qc, kc, vc, kc_prev, vc_prev, ub, st0, p, *,
           layer, prev_layer, rows, chunk, from_cache):
    b, t, _ = h.shape
    n_ub = B_WIDTH // LANES
    assert t % rows == 0 and rows % chunk == 0 and rows % SUBLANES == 0
    assert (rows // SUBLANES) & (rows // SUBLANES - 1) == 0
    row = lambda w: pl.BlockSpec((None, rows, w), lambda bi, i: (bi, i, 0))
    if from_cache:
        assert t == rows
        prev = lambda n, w: pl.BlockSpec((None, None, n, w), lambda bi, i: (prev_layer, bi, 0, 0))
    else:
        assert rows % A_BAND == 0
        prev = lambda n, w: pl.BlockSpec(
            (None, None, n, w), lambda bi, i: (prev_layer, bi, jnp.maximum(i * (rows // n) - 1, 0), 0))
    state_in = pl.BlockSpec((None, None, 2, N_STATE), lambda bi, i: (prev_layer, bi, 0, 0))
    state = pl.BlockSpec((None, 2, N_STATE), lambda bi, i: (bi, 0, 0))
    n_chunks = rows // chunk
    assert (2 * rows // SUBLANES) % n_chunks == 0
    slots = min(2, n_chunks)
    kernel = functools.partial(_mixer_kernel, rows=rows, chunk=chunk, masked=not from_cache, layer=layer)
    return pl.pallas_call(
        kernel,
        grid=(b, t // rows),
        in_specs=[pl.BlockSpec(memory_space=pltpu.SMEM),
                  row(D_MODEL), row(A_WIDTH), prev(A_BAND, A_WIDTH), row(A_WIDTH), prev(A_BAND, A_WIDTH), row(A_WIDTH),
                  _param_block((None,) + bias.shape[1:], (layer, 0, 0, 0)),
                  row(C_WIDTH), prev(C_BAND, C_KV_WIDTH), row(C_KV_WIDTH), prev(C_BAND, C_KV_WIDTH), row(C_KV_WIDTH),
                  pl.BlockSpec((None, n_ub, rows, LANES), lambda bi, i: (bi, 0, i, 0)),
                  state_in,
                  _param_block((None, 2, N_STATE), (layer, 0, 0)),
                  _param_block((None, B_WIDTH, 2 * N_STATE), (layer, 0, 0)),
                  _param_block((None, 2 * N_STATE, B_WIDTH), (layer, 0, 0)),
                  _param_block((None, 1, B_WIDTH), (layer, 0, 0)),
                  _param_block((None, B_WIDTH, 2 * B_WIDTH), (layer, 0, 0)),
                  _param_block((None, 1, D_MIX), (layer, 0, 0)),
                  _param_block((None, D_MIX, D_MODEL), (layer, 0, 0)),
                  _param_block((None, None, 1, D_MODEL), (layer, 3, 0, 0))],
        out_specs=[row(D_MODEL), state],
        out_shape=[jax.ShapeDtypeStruct((b, t, D_MODEL), F32),
                   jax.ShapeDtypeStruct((b, 2, N_STATE), F32)],
        scratch_shapes=[pltpu.VMEM((A_BAND + rows, A_WIDTH), BF16),
                        pltpu.VMEM((A_BAND + rows, 2 * A_WIDTH), BF16),
                        pltpu.VMEM((C_BAND + rows, C_KV_WIDTH), BF16),
                        pltpu.VMEM((C_BAND + rows, C_KV_WIDTH), BF16),
                        pltpu.VMEM((slots, N_SLABS, A_BAND + chunk, 2 * chunk), F32),
                        pltpu.VMEM((slots, N_SLABS, C_BAND + chunk, 2 * chunk), F32),
                        pltpu.VMEM((slots, N_SLABS, A_BAND + chunk, 2 * chunk), BF16),
                        pltpu.VMEM((slots, N_SLABS, C_BAND + chunk, 2 * chunk), BF16),
                        pltpu.VMEM((rows, A_WIDTH + C_WIDTH), F32),
                        pltpu.VMEM((rows, D_MIX), BF16),
                        pltpu.VMEM((rows, B_WIDTH), F32),
                        pltpu.VMEM((rows, 2 * N_STATE), F32),
                        pltpu.VMEM((rows, 2 * N_STATE), F32),
                        pltpu.VMEM((n_ub, rows, LANES), F32),
                        pltpu.VMEM((2, N_STATE), F32),
                        pltpu.VMEM((2, SUBLANES, N_STATE), F32)],
        compiler_params=pltpu.CompilerParams(dimension_semantics=("parallel", "arbitrary"),
                                             vmem_limit_bytes=VMEM_LIMIT_BYTES),
        name="mixer",
    )(p['sink'], h, qa, ka_prev, ka, va_prev, va, bias, qc, kc_prev, kc, vc_prev, vc, ub, st0,
      p['lam'], p['wb'], p['wc'], p['dsk'], p['wglu'], p['gb'], p['wout'], p['norm_g'])


def _rope_table(pos):
    half = ROPE_DIM // 2
    inv_freq = ROPE_THETA ** (-np.arange(half, dtype=np.float64) / half)
    ang = np.asarray(pos, np.float64)[:, None] * inv_freq[None, :]
    cos, sin = jnp.asarray(np.cos(ang), F32), jnp.asarray(np.sin(ang), F32)
    t = len(pos)
    ones = jnp.ones((t, HEAD_DIM - ROPE_DIM), F32)
    zeros = jnp.zeros((t, HEAD_DIM - ROPE_DIM), F32)
    zh = jnp.zeros((t, half), F32)
    head = lambda *parts: jnp.concatenate(parts + parts, axis=-1)
    return jnp.stack([head(cos, cos, ones), head(-sin, zh, zeros), head(zh, sin, zeros)])


def _rel_bias(table, chunk):
    n_heads = table.shape[0]
    period = A_BAND + 2 * CHUNK
    top = table[:, 2 * REL_CLIP:]
    ramp = jnp.flip(table[:, REL_CLIP - CHUNK + 1:2 * REL_CLIP], axis=1)
    n_far = A_BAND - REL_CLIP + 1
    row = jnp.concatenate([jnp.broadcast_to(top, (n_heads, n_far)), ramp,
                           jnp.broadcast_to(top, (n_heads, period - n_far - ramp.shape[1]))], axis=1)
    rolled = jnp.tile(row, (1, CHUNK))[:, :CHUNK * (period - 1)].reshape(n_heads, CHUNK, period - 1)
    return rolled[:, :chunk, :A_BAND + chunk]


def _rel_bias_pairs(tables, chunk):
    depth = tables.shape[0]
    band = A_BAND + chunk
    bias = _rel_bias(tables.reshape(depth * A_HEADS, -1), chunk).reshape(depth, N_SLABS, 2, chunk, band)
    return jnp.transpose(bias, (0, 1, 4, 2, 3)).reshape(depth, N_SLABS, band, 2 * chunk)


def _permute_c_heads(a, axis):
    shape = a.shape
    a = a.reshape(shape[:axis] + (C_KV_HEADS, C_GROUP, HEAD_DIM) + shape[axis + 1:])
    return jnp.swapaxes(a, axis, axis + 1).reshape(shape)


def _prep_params(norm_g, branch_norm_g, w_ffn_up, w_ffn_down, w_in, w_out, a_rel_bias, c_sink,
                 ssm_a_re, ssm_a_im, ssm_log_dt, ssm_b_re, ssm_b_im, ssm_c_re, ssm_c_im, ssm_d, w_glu):
    depth = norm_g.shape[0]
    n_ab = A_WIDTH + B_WIDTH
    win = jnp.concatenate([w_in[:, :, :_O_QC], _permute_c_heads(w_in[:, :, _O_QC:_O_KC], 2), w_in[:, :, _O_KC:]],
                          axis=2)
    gb = jnp.concatenate([branch_norm_g[:, :n_ab], _permute_c_heads(branch_norm_g[:, n_ab:], 1)], axis=1)
    wout = jnp.concatenate([w_out[:, :n_ab], _permute_c_heads(w_out[:, n_ab:], 1)], axis=1)
    lr, li, bbr, bbi = _ssm_prep(ssm_a_re, ssm_a_im, ssm_log_dt, ssm_b_re, ssm_b_im)
    wb = jnp.concatenate([_block_diag(bbr), _block_diag(bbi)], axis=2)
    ct = lambda c: jnp.transpose(c, (0, 1, 3, 2))
    wc = jnp.concatenate([_block_diag(ct(ssm_c_re)), -_block_diag(ct(ssm_c_im))], axis=1)
    return dict(
        norm_g=norm_g[:, :, None, :], wup=w_ffn_up.astype(BF16), wdn=w_ffn_down.astype(BF16),
        win=win.astype(BF16), wout=wout.astype(BF16), gb=gb[:, None, :],
        table=a_rel_bias, sink=c_sink.reshape(depth, C_HEADS),
        lam=jnp.stack([lr.reshape(depth, N_STATE), li.reshape(depth, N_STATE)], axis=1),
        wb=wb.astype(BF16), wc=wc.astype(BF16), dsk=ssm_d.reshape(depth, 1, B_WIDTH),
        wglu=w_glu.astype(BF16))


def _layer(x, rope_tab, p, layer, bias, cache, *, tm_ffn, tm_proj, rows, chunk):
    b, t, _ = x.shape
    ffn = lambda a, which: _ffn(a.reshape(b * t, D_MODEL), p['norm_g'], p['wup'], p['wdn'], layer, which,
                                tm_ffn).reshape(b, t, D_MODEL)
    h = ffn(x, 0)
    qa, ka, va, ub, qc, kc, vc = _proj(h, p['norm_g'], p['win'], rope_tab, layer, tm_proj)
    if cache is None:
        ka_prev, va_prev, kc_prev, vc_prev = ka[None], va[None], kc[None], vc[None]
        st0 = jnp.zeros((1, b, 2, N_STATE), F32)
        prev_layer = 0
    else:
        ka_prev, va_prev, kc_prev, vc_prev, st0 = cache
        prev_layer = layer
    h, st = _mixer(h, qa, ka, va, ka_prev, va_prev, bias, qc, kc, vc, kc_prev, vc_prev, ub, st0, p,
                   layer=layer, prev_layer=prev_layer, rows=rows, chunk=chunk, from_cache=cache is not None)
    h = ffn(h, 1)
    if cache is None:
        nak, nav = ka[:, t - A_BAND:], va[:, t - A_BAND:]
        nck, ncv = kc[:, t - C_BAND:], vc[:, t - C_BAND:]
    else:
        nak = jnp.concatenate([ka_prev[layer, :, t:], ka], axis=1)
        nav = jnp.concatenate([va_prev[layer, :, t:], va], axis=1)
        nck = jnp.concatenate([kc_prev[layer, :, t:], kc], axis=1)
        ncv = jnp.concatenate([vc_prev[layer, :, t:], vc], axis=1)
    states = (nak.reshape(b, A_BAND, A_HEADS, HEAD_DIM), nav.reshape(b, A_BAND, A_HEADS, HEAD_DIM),
              nck.reshape(b, C_BAND, C_KV_HEADS, HEAD_DIM), ncv.reshape(b, C_BAND, C_KV_HEADS, HEAD_DIM),
              st[:, 0].reshape(b, B_GROUPS, B_STATE), st[:, 1].reshape(b, B_GROUPS, B_STATE))
    return h, states


def kernel(x_prompt, x_sample, cache_a_k, cache_a_v, cache_c_k, cache_c_v, state_ssm_re, state_ssm_im, norm_g, branch_norm_g, w_ffn_up, w_ffn_down, w_in, w_out, a_rel_bias, c_sink, ssm_a_re, ssm_a_im, ssm_log_dt, ssm_b_re, ssm_b_im, ssm_c_re, ssm_c_im, ssm_d, w_glu):
    depth = norm_g.shape[0]
    t_p, t_s = x_prompt.shape[1], x_sample.shape[1]
    assert t_p >= A_BAND and t_p % A_BAND == 0 and t_s <= CHUNK
    assert cache_a_k.shape[2] == A_BAND and cache_c_k.shape[2] == C_BAND
    rope_p = _rope_table(np.arange(t_p))
    rope_s = _rope_table(PAST_LEN + np.arange(t_s))
    b_s = x_sample.shape[0]
    p = _prep_params(norm_g, branch_norm_g, w_ffn_up, w_ffn_down, w_in, w_out, a_rel_bias, c_sink,
                     ssm_a_re, ssm_a_im, ssm_log_dt, ssm_b_re, ssm_b_im, ssm_c_re, ssm_c_im, ssm_d, w_glu)
    bias_p = _rel_bias_pairs(p['table'], CHUNK)
    bias_s = _rel_bias_pairs(p['table'], t_s)
    cache = (cache_a_k.reshape(depth, b_s, A_BAND, A_WIDTH), cache_a_v.reshape(depth, b_s, A_BAND, A_WIDTH),
             cache_c_k.reshape(depth, b_s, C_BAND, C_KV_WIDTH), cache_c_v.reshape(depth, b_s, C_BAND, C_KV_WIDTH),
             jnp.stack([state_ssm_re.reshape(depth, b_s, N_STATE), state_ssm_im.reshape(depth, b_s, N_STATE)],
                       axis=2))
    yp, ys = x_prompt, x_sample
    st_p, st_s = [], []
    for l in range(depth):
        yp, sp = _layer(yp, rope_p, p, l, bias_p, None, tm_ffn=512, tm_proj=512, rows=A_BAND, chunk=CHUNK)
        ys, ss = _layer(ys, rope_s, p, l, bias_s, cache, tm_ffn=b_s * t_s, tm_proj=t_s, rows=t_s, chunk=t_s)
        st_p.append(sp)
        st_s.append(ss)
    stack = lambda sts, k: jnp.stack([s[k] for s in sts])
    return (yp, ys) + tuple(stack(st_p, k) for k in range(6)) + tuple(stack(st_s, k) for k in range(6))
```

```python
import functools
import math

import jax
import jax.numpy as jnp
import numpy as np
from jax import lax
from jax.experimental import pallas as pl
from jax.experimental.pallas import tpu as pltpu

F32 = jnp.float32
BF16 = jnp.bfloat16

D_MODEL = 1024
HEAD_DIM = 64
CHUNK = 64
A_HEADS = 6
A_WIDTH = A_HEADS * HEAD_DIM
A_BAND = 8 * CHUNK
REL_CLIP = 128
B_GROUPS = 16
B_GROUP_CH = 16
B_WIDTH = B_GROUPS * B_GROUP_CH
B_STATE = 64
N_STATE = B_GROUPS * B_STATE
C_KV_HEADS = 2
C_GROUP = 3
C_HEADS = C_KV_HEADS * C_GROUP
C_WIDTH = C_HEADS * HEAD_DIM
N_SLABS = 3
MOVE_UNROLL = 4
C_KV_WIDTH = C_KV_HEADS * HEAD_DIM
C_BAND = 128
ROPE_THETA = 500000.0
ROPE_DIM = HEAD_DIM // 4
D_MIX = A_WIDTH + B_WIDTH + C_WIDTH
D_IN = 3 * A_WIDTH + B_WIDTH + C_WIDTH + 2 * C_KV_WIDTH
D_FF = 2816
EPS = 1e-6
NEG_INF = -1e30
ATTN_SCALE = HEAD_DIM ** -0.5
PAST_LEN = 1024

LANES = 128
SUBLANES = 8
VMEM_LIMIT_BYTES = 56 * 1024 * 1024

_O_QA, _O_KA, _O_VA = 0, A_WIDTH, 2 * A_WIDTH
_O_UB = 3 * A_WIDTH
_O_QC = _O_UB + B_WIDTH
_O_KC = _O_QC + C_WIDTH
_O_VC = _O_KC + C_KV_WIDTH


def _rms(x, g):
    return x * lax.rsqrt(jnp.mean(x * x, axis=-1, keepdims=True) + EPS) * g


def _param_block(block_shape, index):
    assert len(block_shape) == len(index)
    return pl.BlockSpec(block_shape, lambda *_: index, pipeline_mode=pl.Buffered(1))


def _ffn_kernel(x_ref, g_pre_ref, g_post_ref, wup_ref, wdn_ref, o_ref):
    x = x_ref[...]
    xn = _rms(x, g_pre_ref[...]).astype(BF16)
    mid = jnp.dot(xn, wup_ref[...], preferred_element_type=F32)
    gate, up = mid[:, :D_FF], mid[:, D_FF:]
    act = (gate * jax.nn.sigmoid(gate) * up).astype(BF16)
    y = jnp.dot(act, wdn_ref[...], preferred_element_type=F32)
    o_ref[...] = x + 0.5 * _rms(y, g_post_ref[...])


def _ffn(x2d, norm_g, wup, wdn, layer, which, tm):
    n = x2d.shape[0]
    gain = lambda k: _param_block((None, None, 1, D_MODEL), (layer, k, 0, 0))
    return pl.pallas_call(
        _ffn_kernel,
        grid=(n // tm,),
        in_specs=[pl.BlockSpec((tm, D_MODEL), lambda i: (i, 0)),
                  gain(4 * which), gain(4 * which + 1),
                  _param_block((None, None, D_MODEL, 2 * D_FF), (layer, which, 0, 0)),
                  _param_block((None, None, D_FF, D_MODEL), (layer, which, 0, 0))],
        out_specs=pl.BlockSpec((tm, D_MODEL), lambda i: (i, 0)),
        out_shape=jax.ShapeDtypeStruct((n, D_MODEL), F32),
        compiler_params=pltpu.CompilerParams(dimension_semantics=("parallel",),
                                             vmem_limit_bytes=VMEM_LIMIT_BYTES),
        name="ffn",
    )(x2d, norm_g, norm_g, wup, wdn)


def _rope_slab(x, cos, sin_hi, sin_lo):
    half = ROPE_DIM // 2
    return (x * cos + pltpu.roll(x, LANES - half, 1) * sin_hi + pltpu.roll(x, half, 1) * sin_lo)


def _proj_kernel(h_ref, g_ref, win_ref, rope_ref,
                 qa_ref, ka_ref, va_ref, ub_ref, qc_ref, kc_ref, vc_ref):
    hn = _rms(h_ref[...], g_ref[...]).astype(BF16)
    p = jnp.dot(hn, win_ref[...], preferred_element_type=F32)
    cos, sin_hi, sin_lo = rope_ref[0], rope_ref[1], rope_ref[2]
    qa_ref[...] = (p[:, _O_QA:_O_QA + A_WIDTH] * ATTN_SCALE).astype(BF16)
    ka_ref[...] = p[:, _O_KA:_O_KA + A_WIDTH]
    va_ref[...] = p[:, _O_VA:_O_VA + A_WIDTH]
    for s in range(B_WIDTH // LANES):
        ub_ref[s] = p[:, _O_UB + s * LANES:_O_UB + (s + 1) * LANES]
    for s in range(C_WIDTH // LANES):
        q = p[:, _O_QC + s * LANES:_O_QC + (s + 1) * LANES]
        qc_ref[:, s * LANES:(s + 1) * LANES] = (_rope_slab(q, cos, sin_hi, sin_lo) * ATTN_SCALE).astype(BF16)
    kc_ref[...] = _rope_slab(p[:, _O_KC:_O_KC + C_KV_WIDTH], cos, sin_hi, sin_lo)
    vc_ref[...] = p[:, _O_VC:_O_VC + C_KV_WIDTH]


def _proj(h, norm_g, win, rope_tab, layer, tm):
    b, t, _ = h.shape
    row = lambda w: pl.BlockSpec((None, tm, w), lambda bi, i: (bi, i, 0))
    n_ub = B_WIDTH // LANES
    return pl.pallas_call(
        _proj_kernel,
        grid=(b, t // tm),
        in_specs=[row(D_MODEL),
                  _param_block((None, None, 1, D_MODEL), (layer, 2, 0, 0)),
                  _param_block((None, D_MODEL, D_IN), (layer, 0, 0)),
                  pl.BlockSpec((3, tm, LANES), lambda bi, i: (0, i, 0))],
        out_specs=[row(A_WIDTH), row(A_WIDTH), row(A_WIDTH),
                   pl.BlockSpec((None, n_ub, tm, LANES), lambda bi, i: (bi, 0, i, 0)),
                   row(C_WIDTH), row(C_KV_WIDTH), row(C_KV_WIDTH)],
        out_shape=[jax.ShapeDtypeStruct((b, t, A_WIDTH), BF16),
                   jax.ShapeDtypeStruct((b, t, A_WIDTH), F32),
                   jax.ShapeDtypeStruct((b, t, A_WIDTH), F32),
                   jax.ShapeDtypeStruct((b, n_ub, t, LANES), F32),
                   jax.ShapeDtypeStruct((b, t, C_WIDTH), BF16),
                   jax.ShapeDtypeStruct((b, t, C_KV_WIDTH), F32),
                   jax.ShapeDtypeStruct((b, t, C_KV_WIDTH), F32)],
        compiler_params=pltpu.CompilerParams(dimension_semantics=("parallel", "parallel"),
                                             vmem_limit_bytes=VMEM_LIMIT_BYTES),
        name="proj",
    )(h, norm_g, win, rope_tab)


def _ssm_prep_kernel(ar_ref, ai_ref, ldt_ref, br_ref, bi_ref, lr_ref, li_ref, bbr_ref, bbi_ref):
    ar, ai = ar_ref[...], ai_ref[...]
    dt = jnp.exp(ldt_ref[...])
    mag = jnp.exp(ar * dt)
    lr, li = mag * jnp.cos(ai * dt), mag * jnp.sin(ai * dt)
    den = ar * ar + ai * ai
    zr = ((lr - 1.0) * ar + li * ai) / den
    zi = (li * ar - (lr - 1.0) * ai) / den
    br, bi = br_ref[...], bi_ref[...]
    lr_ref[...] = lr
    li_ref[...] = li
    bbr_ref[...] = zr * br - zi * bi
    bbi_ref[...] = zr * bi + zi * br


def _ssm_prep(a_re, a_im, log_dt, b_re, b_im):
    depth = a_re.shape[0]
    rows = depth * B_GROUPS * B_GROUP_CH
    rep = lambda a: jnp.repeat(a, B_GROUP_CH, axis=1).reshape(rows, a.shape[-1])
    bt = lambda b: jnp.transpose(b, (0, 1, 3, 2)).reshape(rows, B_STATE)
    sds = jax.ShapeDtypeStruct((rows, B_STATE), F32)
    lr, li, bbr, bbi = pl.pallas_call(
        _ssm_prep_kernel, out_shape=[sds, sds, sds, sds], name="ssm_prep",
    )(rep(a_re), rep(a_im), rep(log_dt[:, :, None]), bt(b_re), bt(b_im))
    per_pair = lambda m: m.reshape(depth, B_GROUPS, B_GROUP_CH, B_STATE)
    return per_pair(lr)[:, :, 0], per_pair(li)[:, :, 0], per_pair(bbr), per_pair(bbi)


def _block_diag(m):
    depth, _, r, c = m.shape
    eye = jnp.eye(B_GROUPS, dtype=m.dtype)
    full = m[:, :, :, None, :] * eye[None, :, None, :, None]
    return full.reshape(depth, B_GROUPS * r, B_GROUPS * c)


def _mixer_kernel(sink_ref,
                  h_ref, qa_ref, kap_ref, kac_ref, vap_ref, vac_ref, bias_ref,
                  qc_ref, kcp_ref, kcc_ref, vcp_ref, vcc_ref,
                  ub_ref, st0_ref, lam_ref, wb_ref, wc_ref, dsk_ref, wglu_ref,
                  gb_ref, wout_ref, g3_ref,
                  o_ref, st_ref,
                  kbuf, vbuf, kcbuf, vcbuf, sa_scr, sc_scr, pa_scr, pc_scr, att_scr,
                  mix_scr, u_scr, v_scr, x_scr, nb_scr, carry_scr, scan_scr,
                  *, rows, chunk, masked, layer):
    i = pl.program_id(1)
    n_chunks = rows // chunk
    band_a = A_BAND + chunk
    band_c = C_BAND + chunk

    kbuf[0:A_BAND, :] = kap_ref[...].astype(BF16)
    kbuf[A_BAND:A_BAND + rows, :] = kac_ref[...].astype(BF16)
    for s in range(N_SLABS):
        vbuf[0:A_BAND, 2 * s * LANES:(2 * s + 1) * LANES] = vap_ref[:, s * LANES:(s + 1) * LANES].astype(BF16)
        vbuf[A_BAND:A_BAND + rows, 2 * s * LANES:(2 * s + 1) * LANES] = (
            vac_ref[:, s * LANES:(s + 1) * LANES].astype(BF16))
        vbuf[:, (2 * s + 1) * LANES:(2 * s + 2) * LANES] = jnp.ones((A_BAND + rows, LANES), BF16)
    kcbuf[0:C_BAND, :] = kcp_ref[...].astype(BF16)
    kcbuf[C_BAND:C_BAND + rows, :] = kcc_ref[...].astype(BF16)
    vcbuf[0:C_BAND, :] = vcp_ref[...].astype(BF16)
    vcbuf[C_BAND:C_BAND + rows, :] = vcc_ref[...].astype(BF16)

    lane = lax.broadcasted_iota(jnp.int32, (1, LANES), 1)
    first_head = lane < HEAD_DIM
    contract_last = (((1,), (1,)), ((), ()))

    contract_first = (((0,), (0,)), ((), ()))
    first_half = lax.broadcasted_iota(jnp.int32, (1, 2 * chunk), 1) < chunk

    def score_chunk(c, slot, mask_early):
        r0 = c * chunk
        q_pos = i * rows + c * chunk

        def pair_scores(q_ref, keys, s, back, bias):
            q = q_ref[pl.ds(r0, chunk), s * LANES:(s + 1) * LANES]
            zero = jnp.zeros_like(q)
            both = jnp.concatenate([jnp.where(first_head, q, zero), jnp.where(first_head, zero, q)], axis=0)
            st = lax.dot_general(keys, both, contract_last, preferred_element_type=F32)
            if bias is not None:
                st = st + bias
            if mask_early:
                st = jnp.where(lax.broadcasted_iota(jnp.int32, st.shape, 0) >= back - q_pos, st, NEG_INF)
            return st

        for s in range(N_SLABS):
            sa_scr[slot, s] = pair_scores(qa_ref, kbuf[pl.ds(r0, band_a), s * LANES:(s + 1) * LANES], s, A_BAND,
                                          bias_ref[s])
        keys = kcbuf[pl.ds(r0, band_c), :]
        if 2 * chunk == LANES:
            q = qc_ref[pl.ds(r0, chunk), :]
            zero = jnp.zeros((chunk, LANES), q.dtype)
            parts = []
            for s in range(N_SLABS):
                qs = q[:, s * LANES:(s + 1) * LANES]
                parts += [jnp.where(first_head, qs, zero), jnp.where(first_head, zero, qs)]
            st = lax.dot_general(keys, jnp.concatenate(parts, axis=0), contract_last, preferred_element_type=F32)
            if mask_early:
                st = jnp.where(lax.broadcasted_iota(jnp.int32, st.shape, 0) >= C_BAND - q_pos, st, NEG_INF)
            for s in range(N_SLABS):
                sc_scr[slot, s] = st[:, s * LANES:(s + 1) * LANES]
        else:
            for s in range(N_SLABS):
                sc_scr[slot, s] = pair_scores(qc_ref, keys, s, C_BAND, None)

    def softmax_chunk(slot):
        for s in range(N_SLABS):
            m = jnp.max(sa_scr[slot, s], axis=0, keepdims=True)
            pa_scr[slot, s] = jnp.exp(sa_scr[slot, s] - m).astype(BF16)
        for s in range(N_SLABS):
            st = sc_scr[slot, s]
            sink = jnp.where(first_half, sink_ref[layer, s], sink_ref[layer, C_GROUP + s])
            m = jnp.maximum(jnp.max(st, axis=0, keepdims=True), sink)
            e = jnp.exp(st - m)
            den = jnp.sum(e, axis=0, keepdims=True) + jnp.exp(sink - m)
            pc_scr[slot, s] = (e * (1.0 / den)).astype(BF16)

    def value_chunk(c, slot):
        r0 = c * chunk

        def merge_heads(o):
            return jnp.where(first_head, o[:chunk], o[chunk:])

        for s in range(N_SLABS):
            o2 = lax.dot_general(pa_scr[slot, s], vbuf[pl.ds(r0, band_a), 2 * s * LANES:(2 * s + 2) * LANES],
                                 contract_first, preferred_element_type=F32)
            att_scr[pl.ds(r0, chunk), s * LANES:(s + 1) * LANES] = merge_heads(o2[:, :LANES] * (1.0 / o2[:, LANES:]))
        vals = vcbuf[pl.ds(r0, band_c), :]
        for s in range(N_SLABS):
            o = lax.dot_general(pc_scr[slot, s], vals, contract_first, preferred_element_type=F32)
            att_scr[pl.ds(r0, chunk), A_WIDTH + s * LANES:A_WIDTH + (s + 1) * LANES] = merge_heads(o)

    seg = rows // SUBLANES
    steps_per_chunk = (2 * seg) // n_chunks

    @pl.when(i == 0)
    def _():
        carry_scr[...] = st0_ref[...]

    def gather_u(t, carry):
        t8 = pl.multiple_of(t * SUBLANES, SUBLANES)
        for s in range(B_WIDTH // LANES):
            u_scr[pl.ds(t8, SUBLANES), s * LANES:(s + 1) * LANES] = ub_ref[s, pl.ds(t, SUBLANES, stride=seg), :]
        return carry

    lax.fori_loop(0, seg, gather_u, 0, unroll=min(seg, MOVE_UNROLL))
    v_scr[...] = jnp.dot(u_scr[...].astype(BF16), wb_ref[...], preferred_element_type=F32)

    def segment_starts(end_r, end_i):
        lam_r, lam_i = lam_ref[0:1, :], lam_ref[1:2, :]
        pw_r, pw_i = lam_r, lam_i
        for _ in range(int(math.log2(seg))):
            pw_r, pw_i = pw_r * pw_r - pw_i * pw_i, 2.0 * pw_r * pw_i
        cr, ci = carry_scr[0:1, :], carry_scr[1:2, :]
        start_r, start_i = [], []
        for j in range(SUBLANES):
            start_r.append(cr)
            start_i.append(ci)
            cr, ci = (end_r[j:j + 1, :] + pw_r * cr - pw_i * ci,
                      end_i[j:j + 1, :] + pw_r * ci + pw_i * cr)
        carry_scr[0:1, :] = cr
        carry_scr[1:2, :] = ci
        st_ref[0:1, :] = cr
        st_ref[1:2, :] = ci
        return jnp.concatenate(start_r, axis=0), jnp.concatenate(start_i, axis=0)

    def scan_share(k):
        lam_r8 = jnp.broadcast_to(lam_ref[0:1, :], (SUBLANES, N_STATE))
        lam_i8 = jnp.broadcast_to(lam_ref[1:2, :], (SUBLANES, N_STATE))
        if k == 0:
            xr = xi = jnp.zeros((SUBLANES, N_STATE), F32)
        else:
            xr, xi = scan_scr[0], scan_scr[1]
        for idx in range(k * steps_per_chunk, (k + 1) * steps_per_chunk):
            if idx == seg:
                xr, xi = segment_starts(xr, xi)
            t8 = (idx % seg) * SUBLANES
            vr = v_scr[t8:t8 + SUBLANES, 0:N_STATE]
            vi = v_scr[t8:t8 + SUBLANES, N_STATE:2 * N_STATE]
            xr, xi = lam_r8 * xr - lam_i8 * xi + vr, lam_r8 * xi + lam_i8 * xr + vi
            if idx >= seg:
                x_scr[t8:t8 + SUBLANES, 0:N_STATE] = xr
                x_scr[t8:t8 + SUBLANES, N_STATE:2 * N_STATE] = xi
        if k + 1 < n_chunks:
            scan_scr[0] = xr
            scan_scr[1] = xi

    def run_chunks(mask_early):
        score_chunk(0, 0, mask_early)
        for c in range(n_chunks):
            if c + 1 < n_chunks:
                score_chunk(c + 1, (c + 1) % 2, mask_early)
            softmax_chunk(c % 2)
            value_chunk(c, c % 2)
            scan_share(c)

    if masked:
        pl.when(i == 0)(lambda: run_chunks(True))
        pl.when(i != 0)(lambda: run_chunks(False))
    else:
        run_chunks(False)
    mix_scr[:, 0:A_WIDTH] = _rms(att_scr[:, 0:A_WIDTH], gb_ref[:, 0:A_WIDTH]).astype(BF16)
    mix_scr[:, A_WIDTH + B_WIDTH:D_MIX] = _rms(att_scr[:, A_WIDTH:A_WIDTH + C_WIDTH],
                                               gb_ref[:, A_WIDTH + B_WIDTH:D_MIX]).astype(BF16)

    y = jnp.dot(x_scr[...].astype(BF16), wc_ref[...], preferred_element_type=F32) + dsk_ref[...] * u_scr[...]
    gl = jax.nn.gelu(y, approximate=True)
    g12 = jnp.dot(gl.astype(BF16), wglu_ref[...], preferred_element_type=F32)
    ob = g12[:, :B_WIDTH] * jax.nn.sigmoid(g12[:, B_WIDTH:])
    u_scr[...] = _rms(ob, gb_ref[:, A_WIDTH:A_WIDTH + B_WIDTH])

    def scatter_nb(t, carry):
        t8 = pl.multiple_of(t * SUBLANES, SUBLANES)
        for s in range(B_WIDTH // LANES):
            nb_scr[s, pl.ds(t, SUBLANES, stride=seg), :] = u_scr[pl.ds(t8, SUBLANES), s * LANES:(s + 1) * LANES]
        return carry

    lax.fori_loop(0, seg, scatter_nb, 0)
    for s in range(B_WIDTH // LANES):
        mix_scr[:, A_WIDTH + s * LANES:A_WIDTH + (s + 1) * LANES] = nb_scr[s].astype(BF16)

    out = jnp.dot(mix_scr[...], wout_ref[...], preferred_element_type=F32)
    o_ref[...] = h_ref[...] + _rms(out, g3_ref[...])


def _mixer(h, qa, ka, va, ka_prev, va_prev, bias, qc, kc, vc, kc_prev, vc_prev, ub, st0, p, *,
           layer, prev_layer, rows, chunk, from_cache):
    b, t, _ = h.shape
    n_ub = B_WIDTH // LANES
    assert t % rows == 0 and rows % chunk == 0 and rows % SUBLANES == 0
    assert (rows // SUBLANES) & (rows // SUBLANES - 1) == 0
    row = lambda w: pl.BlockSpec((None, rows, w), lambda bi, i: (bi, i, 0))
    if from_cache:
        assert t == rows
        prev = lambda n, w: pl.BlockSpec((None, None, n, w), lambda bi, i: (prev_layer, bi, 0, 0))
    else:
        assert rows % A_BAND == 0
        prev = lambda n, w: pl.BlockSpec(
            (None, None, n, w), lambda bi, i: (prev_layer, bi, jnp.maximum(i * (rows // n) - 1, 0), 0))
    state_in = pl.BlockSpec((None, None, 2, N_STATE), lambda bi, i: (prev_layer, bi, 0, 0))
    state = pl.BlockSpec((None, 2, N_STATE), lambda bi, i: (bi, 0, 0))
    n_chunks = rows // chunk
    assert (2 * rows // SUBLANES) % n_chunks == 0
    slots = min(2, n_chunks)
    kernel = functools.partial(_mixer_kernel, rows=rows, chunk=chunk, masked=not from_cache, layer=layer)
    return pl.pallas_call(
        kernel,
        grid=(b, t // rows),
        in_specs=[pl.BlockSpec(memory_space=pltpu.SMEM),
                  row(D_MODEL), row(A_WIDTH), prev(A_BAND, A_WIDTH), row(A_WIDTH), prev(A_BAND, A_WIDTH), row(A_WIDTH),
                  _param_block((None,) + bias.shape[1:], (layer, 0, 0, 0)),
                  row(C_WIDTH), prev(C_BAND, C_KV_WIDTH), row(C_KV_WIDTH), prev(C_BAND, C_KV_WIDTH), row(C_KV_WIDTH),
                  pl.BlockSpec((None, n_ub, rows, LANES), lambda bi, i: (bi, 0, i, 0)),
                  state_in,
                  _param_block((None, 2, N_STATE), (layer, 0, 0)),
                  _param_block((None, B_WIDTH, 2 * N_STATE), (layer, 0, 0)),
                  _param_block((None, 2 * N_STATE, B_WIDTH), (layer, 0, 0)),
                  _param_block((None, 1, B_WIDTH), (layer, 0, 0)),
                  _param_block((None, B_WIDTH, 2 * B_WIDTH), (layer, 0, 0)),
                  _param_block((None, 1, D_MIX), (layer, 0, 0)),
                  _param_block((None, D_MIX, D_MODEL), (layer, 0, 0)),
                  _param_block((None, None, 1, D_MODEL), (layer, 3, 0, 0))],
        out_specs=[row(D_MODEL), state],
        out_shape=[jax.ShapeDtypeStruct((b, t, D_MODEL), F32),
                   jax.ShapeDtypeStruct((b, 2, N_STATE), F32)],
        scratch_shapes=[pltpu.VMEM((A_BAND + rows, A_WIDTH), BF16),
                        pltpu.VMEM((A_BAND + rows, 2 * A_WIDTH), BF16),
                        pltpu.VMEM((C_BAND + rows, C_KV_WIDTH), BF16),
                        pltpu.VMEM((C_BAND + rows, C_KV_WIDTH), BF16),
                        pltpu.VMEM((slots, N_SLABS, A_BAND + chunk, 2 * chunk), F32),
                        pltpu.VMEM((slots, N_SLABS, C_BAND + chunk, 2 * chunk), F32),
                        pltpu.VMEM((slots, N_SLABS, A_BAND + chunk, 2 * chunk), BF16),
                        pltpu.VMEM((slots, N_SLABS, C_BAND + chunk, 2 * chunk), BF16),
                        pltpu.VMEM((rows, A_WIDTH + C_WIDTH), F32),
                        pltpu.VMEM((rows, D_MIX), BF16),
                        pltpu.VMEM((rows, B_WIDTH), F32),
                        pltpu.VMEM((rows, 2 * N_STATE), F32),
                        pltpu.VMEM((rows, 2 * N_STATE), F32),
                        pltpu.VMEM((n_ub, rows, LANES), F32),
                        pltpu.VMEM((2, N_STATE), F32),
                        pltpu.VMEM((2, SUBLANES, N_STATE), F32)],
        compiler_params=pltpu.CompilerParams(dimension_semantics=("parallel", "arbitrary"),
                                             vmem_limit_bytes=VMEM_LIMIT_BYTES),
        name="mixer",
    )(p['sink'], h, qa, ka_prev, ka, va_prev, va, bias, qc, kc_prev, kc, vc_prev, vc, ub, st0,
      p['lam'], p['wb'], p['wc'], p['dsk'], p['wglu'], p['gb'], p['wout'], p['norm_g'])


def _rope_table(pos):
    half = ROPE_DIM // 2
    inv_freq = ROPE_THETA ** (-np.arange(half, dtype=np.float64) / half)
    ang = np.asarray(pos, np.float64)[:, None] * inv_freq[None, :]
    cos, sin = jnp.asarray(np.cos(ang), F32), jnp.asarray(np.sin(ang), F32)
    t = len(pos)
    ones = jnp.ones((t, HEAD_DIM - ROPE_DIM), F32)
    zeros = jnp.zeros((t, HEAD_DIM - ROPE_DIM), F32)
    zh = jnp.zeros((t, half), F32)
    head = lambda *parts: jnp.concatenate(parts + parts, axis=-1)
    return jnp.stack([head(cos, cos, ones), head(-sin, zh, zeros), head(zh, sin, zeros)])


def _rel_bias(table, chunk):
    n_heads = table.shape[0]
    period = A_BAND + 2 * CHUNK
    top = table[:, 2 * REL_CLIP:]
    ramp = jnp.flip(table[:, REL_CLIP - CHUNK + 1:2 * REL_CLIP], axis=1)
    n_far = A_BAND - REL_CLIP + 1
    row = jnp.concatenate([jnp.broadcast_to(top, (n_heads, n_far)), ramp,
                           jnp.broadcast_to(top, (n_heads, period - n_far - ramp.shape[1]))], axis=1)
    rolled = jnp.tile(row, (1, CHUNK))[:, :CHUNK * (period - 1)].reshape(n_heads, CHUNK, period - 1)
    return rolled[:, :chunk, :A_BAND + chunk]


def _rel_bias_pairs(tables, chunk):
    depth = tables.shape[0]
    band = A_BAND + chunk
    bias = _rel_bias(tables.reshape(depth * A_HEADS, -1), chunk).reshape(depth, N_SLABS, 2, chunk, band)
    return jnp.transpose(bias, (0, 1, 4, 2, 3)).reshape(depth, N_SLABS, band, 2 * chunk)


def _permute_c_heads(a, axis):
    shape = a.shape
    a = a.reshape(shape[:axis] + (C_KV_HEADS, C_GROUP, HEAD_DIM) + shape[axis + 1:])
    return jnp.swapaxes(a, axis, axis + 1).reshape(shape)


def _prep_params(norm_g, branch_norm_g, w_ffn_up, w_ffn_down, w_in, w_out, a_rel_bias, c_sink,
                 ssm_a_re, ssm_a_im, ssm_log_dt, ssm_b_re, ssm_b_im, ssm_c_re, ssm_c_im, ssm_d, w_glu):
    depth = norm_g.shape[0]
    n_ab = A_WIDTH + B_WIDTH
    win = jnp.concatenate([w_in[:, :, :_O_QC], _permute_c_heads(w_in[:, :, _O_QC:_O_KC], 2), w_in[:, :, _O_KC:]],
                          axis=2)
    gb = jnp.concatenate([branch_norm_g[:, :n_ab], _permute_c_heads(branch_norm_g[:, n_ab:], 1)], axis=1)
    wout = jnp.concatenate([w_out[:, :n_ab], _permute_c_heads(w_out[:, n_ab:], 1)], axis=1)
    lr, li, bbr, bbi = _ssm_prep(ssm_a_re, ssm_a_im, ssm_log_dt, ssm_b_re, ssm_b_im)
    wb = jnp.concatenate([_block_diag(bbr), _block_diag(bbi)], axis=2)
    ct = lambda c: jnp.transpose(c, (0, 1, 3, 2))
    wc = jnp.concatenate([_block_diag(ct(ssm_c_re)), -_block_diag(ct(ssm_c_im))], axis=1)
    return dict(
        norm_g=norm_g[:, :, None, :], wup=w_ffn_up.astype(BF16), wdn=w_ffn_down.astype(BF16),
        win=win.astype(BF16), wout=wout.astype(BF16), gb=gb[:, None, :],
        table=a_rel_bias, sink=c_sink.reshape(depth, C_HEADS),
        lam=jnp.stack([lr.reshape(depth, N_STATE), li.reshape(depth, N_STATE)], axis=1),
        wb=wb.astype(BF16), wc=wc.astype(BF16), dsk=ssm_d.reshape(depth, 1, B_WIDTH),
        wglu=w_glu.astype(BF16))


def _layer(x, rope_tab, p, layer, bias, cache, *, tm_ffn, tm_proj, rows, chunk):
    b, t, _ = x.shape
    ffn = lambda a, which: _ffn(a.reshape(b * t, D_MODEL), p['norm_g'], p['wup'], p['wdn'], layer, which,
                                tm_ffn).reshape(b, t, D_MODEL)
    h = ffn(x, 0)
    qa, ka, va, ub, qc, kc, vc = _proj(h, p['norm_g'], p['win'], rope_tab, layer, tm_proj)
    if cache is None:
        ka_prev, va_prev, kc_prev, vc_prev = ka[None], va[None], kc[None], vc[None]
        st0 = jnp.zeros((1, b, 2, N_STATE), F32)
        prev_layer = 0
    else:
        ka_prev, va_prev, kc_prev, vc_prev, st0 = cache
        prev_layer = layer
    h, st = _mixer(h, qa, ka, va, ka_prev, va_prev, bias, qc, kc, vc, kc_prev, vc_prev, ub, st0, p,
                   layer=layer, prev_layer=prev_layer, rows=rows, chunk=chunk, from_cache=cache is not None)
    h = ffn(h, 1)
    if cache is None:
        nak, nav = ka[:, t - A_BAND:], va[:, t - A_BAND:]
        nck, ncv = kc[:, t - C_BAND:], vc[:, t - C_BAND:]
    else:
        nak = jnp.concatenate([ka_prev[layer, :, t:], ka], axis=1)
        nav = jnp.concatenate([va_prev[layer, :, t:], va], axis=1)
        nck = jnp.concatenate([kc_prev[layer, :, t:], kc], axis=1)
        ncv = jnp.concatenate([vc_prev[layer, :, t:], vc], axis=1)
    states = (nak.reshape(b, A_BAND, A_HEADS, HEAD_DIM), nav.reshape(b, A_BAND, A_HEADS, HEAD_DIM),
              nck.reshape(b, C_BAND, C_KV_HEADS, HEAD_DIM), ncv.reshape(b, C_BAND, C_KV_HEADS, HEAD_DIM),
              st[:, 0].reshape(b, B_GROUPS, B_STATE), st[:, 1].reshape(b, B_GROUPS, B_STATE))
    return h, states


def kernel(x_prompt, x_sample, cache_a_k, cache_a_v, cache_c_k, cache_c_v, state_ssm_re, state_ssm_im, norm_g, branch_norm_g, w_ffn_up, w_ffn_down, w_in, w_out, a_rel_bias, c_sink, ssm_a_re, ssm_a_im, ssm_log_dt, ssm_b_re, ssm_b_im, ssm_c_re, ssm_c_im, ssm_d, w_glu):
    depth = norm_g.shape[0]
    t_p, t_s = x_prompt.shape[1], x_sample.shape[1]
    assert t_p >= A_BAND and t_p % A_BAND == 0 and t_s <= CHUNK
    assert cache_a_k.shape[2] == A_BAND and cache_c_k.shape[2] == C_BAND
    rope_p = _rope_table(np.arange(t_p))
    rope_s = _rope_table(PAST_LEN + np.arange(t_s))
    b_s = x_sample.shape[0]
    p = _prep_params(norm_g, branch_norm_g, w_ffn_up, w_ffn_down, w_in, w_out, a_rel_bias, c_sink,
                     ssm_a_re, ssm_a_im, ssm_log_dt, ssm_b_re, ssm_b_im, ssm_c_re, ssm_c_im, ssm_d, w_glu)
    bias_p = _rel_bias_pairs(p['table'], CHUNK)
    bias_s = _rel_bias_pairs(p['table'], t_s)
    cache = (cache_a_k.reshape(depth, b_s, A_BAND, A_WIDTH), cache_a_v.reshape(depth, b_s, A_BAND, A_WIDTH),
             cache_c_k.reshape(depth, b_s, C_BAND, C_KV_WIDTH), cache_c_v.reshape(depth, b_s, C_BAND, C_KV_WIDTH),
             jnp.stack([state_ssm_re.reshape(depth, b_s, N_STATE), state_ssm_im.reshape(depth, b_s, N_STATE)],
                       axis=2))
    yp, ys = x_prompt, x_sample
    st_p, st_s = [], []
    for l in range(depth):
        yp, sp = _layer(yp, rope_p, p, l, bias_p, None, tm_ffn=512, tm_proj=512, rows=A_BAND, chunk=CHUNK)
        ys, ss = _layer(ys, rope_s, p, l, bias_s, cache, tm_ffn=b_s * t_s, tm_proj=t_s, rows=t_s, chunk=t_s)
        st_p.append(sp)
        st_s.append(ss)
    stack = lambda sts, k: jnp.stack([s[k] for s in sts])
    return (yp, ys) + tuple(stack(st_p, k) for k in range(6)) + tuple(stack(st_s, k) for k in range(6))
```
